```python
import math
import jax, jax.numpy as jnp
from jax import lax
import numpy as np

D_MODEL = 1024
BATCH = 1
SEQ = 16384
DEPTH = 2
DEC_BATCH = 2
DEC_SEQ = 8192
PAST_LEN = 128

GRID_W = 64
D_FNET = D_MODEL // 2
N_FNET_GROUPS = 4
D_HYENA = D_MODEL // 2
SHORT_CONV = 3
POS_BANDS = 16
POS_EMB = 1 + 2 * POS_BANDS
FILTER_HIDDEN = 64
DECAY_SLOW = 3.0701
DECAY_FAST = 15.3506
N_HEADS = 16
HEAD_DIM = D_MODEL // N_HEADS
WIN_ROWS_MAX = 8
WIN_COLS = 16
N_EXPERTS = 64
TOP_K = 8
N_GROUPS = 8
TOPK_GROUPS = 4
D_EXPERT = 256
D_SHARED = 256
ROUTED_SCALE = 2.5
EXPERT_CHUNK = 256
EPS = 1e-6

kernel_name = 'hybrid_fnet_hyena_natten_moe_encoder'


def rms_norm(x, g):
    x32 = x.astype(jnp.float32)
    y = x32 * lax.rsqrt(jnp.mean(x32 * x32, axis=-1, keepdims=True) + EPS)
    return (y * g.astype(jnp.float32)).astype(x.dtype)


def ada_modulation(c, w, b):
    mod = jax.nn.silu(c) @ w + b
    return jnp.split(mod[:, None, :], 6, axis=-1)


def fourier_mix(u):
    b, l, _ = u.shape
    ug = u.astype(jnp.float32).reshape(b, l, N_FNET_GROUPS, D_FNET // N_FNET_GROUPS)
    y = jnp.fft.fft2(ug, axes=(1, 3), norm='ortho').real
    return y.reshape(b, l, D_FNET).astype(u.dtype)


def hyena_filters(L, w1, b1, f1, w2, b2, f2, w3, decay):
    f32 = jnp.float32
    t = jnp.linspace(0.0, 1.0, L, dtype=f32)[:, None]
    bands = jnp.linspace(1e-4, POS_BANDS - 1, POS_BANDS, dtype=f32)
    ang = (2.0 * math.pi / L) * jnp.arange(L, dtype=f32)[:, None] * bands
    z = jnp.concatenate([t, jnp.cos(ang), -jnp.sin(ang)], axis=-1)
    h = jnp.sin(f1.astype(f32) * (z @ w1.astype(f32) + b1.astype(f32)))
    h = jnp.sin(f2.astype(f32) * (h @ w2.astype(f32) + b2.astype(f32)))
    h = h @ w3.astype(f32)
    dec = jnp.abs(decay.astype(f32))
    h_fwd = h[:, :D_HYENA] * jnp.exp(-t * dec[0])
    h_bwd = h[:, D_HYENA:] * jnp.exp(-t * dec[1])
    hc = jnp.concatenate([h_fwd, jnp.zeros((1, D_HYENA), f32), h_bwd[1:][::-1]], axis=0)
    hc = hc * lax.rsqrt(jnp.sum(hc * hc, axis=0, keepdims=True) + EPS)
    return jnp.fft.rfft(hc, axis=0)


def hyena_mix(u, conv_w, conv_b, w1, b1, f1, w2, b2, f2, w3, decay, hbias):
    L = u.shape[1]
    half = SHORT_CONV // 2
    up = jnp.pad(u, ((0, 0), (half, half), (0, 0)))
    uc = conv_b + sum(up[:, j:j + L] * conv_w[j] for j in range(SHORT_CONV))
    x0, x1, v = jnp.split(uc, 3, axis=-1)
    z = (v * x1).astype(jnp.float32)
    H = hyena_filters(L, w1, b1, f1, w2, b2, f2, w3, decay)
    zc = jnp.fft.irfft(jnp.fft.rfft(z, n=2 * L, axis=1) * H, n=2 * L, axis=1)[:, :L]
    z = zc + z * hbias.astype(jnp.float32)
    return (x0.astype(jnp.float32) * z).astype(u.dtype)


def fourier_hyena_mixer(h, w_in, w_out, conv_w, conv_b, w1, b1, f1, w2, b2, f2, w3, decay, hbias):
    u = h @ w_in
    y_a = fourier_mix(u[..., :D_FNET])
    y_b = hyena_mix(u[..., D_FNET:], conv_w, conv_b, w1, b1, f1, w2, b2, f2, w3, decay, hbias)
    return jnp.concatenate([y_a, y_b], axis=-1) @ w_out


def neighbourhood_attention(qkv, rpb):
    b, L, _ = qkv.shape
    rows = L // GRID_W
    kh = min(WIN_ROWS_MAX, rows)
    f32 = jnp.float32
    q, k, v = jnp.split(qkv, 3, axis=-1)
    q = q.reshape(b, rows, GRID_W, N_HEADS, HEAD_DIM).transpose(1, 0, 3, 2, 4)
    k = k.reshape(b, rows, GRID_W, N_HEADS, HEAD_DIM).transpose(0, 3, 1, 2, 4)
    v = v.reshape(b, rows, GRID_W, N_HEADS, HEAD_DIM).transpose(0, 3, 1, 2, 4)
    cols = jnp.arange(GRID_W)
    col_start = jnp.clip(cols - WIN_COLS // 2, 0, GRID_W - WIN_COLS)
    col_idx = col_start[:, None] + jnp.arange(WIN_COLS)
    col_bias_idx = col_idx - cols[:, None] + (WIN_COLS - 1)
    rpb32 = rpb.astype(f32)
    scale = HEAD_DIM ** -0.5

    def one_row(args):
        q_r, r = args
        rs = jnp.clip(r - kh // 2, 0, rows - kh)
        k_win = lax.dynamic_slice_in_dim(k, rs, kh, axis=2)[:, :, :, col_idx].astype(f32)
        v_win = lax.dynamic_slice_in_dim(v, rs, kh, axis=2)[:, :, :, col_idx].astype(f32)
        s = jnp.einsum('bhqd,bhrqkd->bhqrk', q_r.astype(f32), k_win) * scale
        row_bias_idx = rs + jnp.arange(kh) - r + (WIN_ROWS_MAX - 1)
        bias = rpb32[:, row_bias_idx][:, :, col_bias_idx]
        s = s + bias.transpose(0, 2, 1, 3)[None]
        p = jax.nn.softmax(s.reshape(b, N_HEADS, GRID_W, kh * WIN_COLS), axis=-1).reshape(s.shape)
        return jnp.einsum('bhqrk,bhrqkd->bhqd', p, v_win).astype(q_r.dtype)

    o = lax.map(one_row, (q, jnp.arange(rows, dtype=jnp.int32)))
    return o.transpose(1, 0, 3, 2, 4).reshape(b, L, N_HEADS * HEAD_DIM)


def routed_experts(xf, idx, wts, w1, w3, w2):
    T, _ = xf.shape
    tk = T * TOP_K
    e_flat = idx.reshape(tk)
    tok_flat = jnp.repeat(jnp.arange(T, dtype=jnp.int32), TOP_K)
    w_flat = wts.reshape(tk)
    order = jnp.argsort(e_flat)
    e_s, tok_s, w_s = e_flat[order], tok_flat[order], w_flat[order]
    counts = jnp.bincount(e_flat, length=N_EXPERTS)
    starts = jnp.cumsum(counts) - counts
    padded = (counts + EXPERT_CHUNK - 1) // EXPERT_CHUNK * EXPERT_CHUNK
    pends = jnp.cumsum(padded)
    pstarts = pends - padded
    dest = pstarts[e_s] + jnp.arange(tk, dtype=jnp.int32) - starts[e_s]
    n_slots = tk + N_EXPERTS * EXPERT_CHUNK
    n_chunks = n_slots // EXPERT_CHUNK
    slot_tok = jnp.zeros((n_slots,), jnp.int32).at[dest].set(tok_s)
    slot_w = jnp.zeros((n_slots,), wts.dtype).at[dest].set(w_s)
    chunk_e = jnp.minimum(
        jnp.searchsorted(pends, jnp.arange(n_chunks, dtype=jnp.int32) * EXPERT_CHUNK, side='right'),
        N_EXPERTS - 1)

    def step(acc, inp):
        tok, w, e = inp
        xc = xf[tok]
        hc = jax.nn.silu(xc @ w1[e]) * (xc @ w3[e])
        return acc.at[tok].add((hc @ w2[e]) * w[:, None]), None

    acc, _ = lax.scan(step, jnp.zeros_like(xf),
                      (slot_tok.reshape(n_chunks, EXPERT_CHUNK),
                       slot_w.reshape(n_chunks, EXPERT_CHUNK), chunk_e))
    return acc


def moe_ffn(h, router_w, router_b, w1, w3, w2, s1, s3, s2):
    b, L, d = h.shape
    xf = h.reshape(b * L, d)
    T = b * L
    f32 = jnp.float32
    scores = jax.nn.sigmoid((xf @ router_w).astype(f32))
    sel = scores + router_b.astype(f32)
    grp_score = lax.top_k(sel.reshape(T, N_GROUPS, N_EXPERTS // N_GROUPS), 2)[0].sum(-1)
    _, gidx = lax.top_k(grp_score, TOPK_GROUPS)
    gmask = jnp.sum(jax.nn.one_hot(gidx, N_GROUPS, dtype=f32), axis=1) > 0
    emask = jnp.repeat(gmask, N_EXPERTS // N_GROUPS, axis=1)
    _, idx = lax.top_k(jnp.where(emask, sel, -jnp.inf), TOP_K)
    wts = jnp.take_along_axis(scores, idx, axis=1)
    wts = wts / jnp.sum(wts, axis=-1, keepdims=True) * ROUTED_SCALE
    routed = routed_experts(xf, idx, wts.astype(xf.dtype), w1, w3, w2)
    shared = (jax.nn.silu(xf @ s1) * (xf @ s3)) @ s2
    return (routed + shared).reshape(b, L, d)


def encoder_trunk(x, c, ada_w, ada_b, g_mix, g_ffn, g_final,
                  w_in_ab, w_out_ab, conv_w, conv_b, filt_w1, filt_b1, filt_f1,
                  filt_w2, filt_b2, filt_f2, filt_w3, decay, hyena_bias,
                  w_qkv, w_out_c, rpb,
                  router_w, router_b, exp_w1, exp_w3, exp_w2, sh_w1, sh_w3, sh_w2):
    for l in range(DEPTH):
        sh_m, sc_m, gt_m, sh_f, sc_f, gt_f = ada_modulation(c, ada_w[l], ada_b[l])
        h = rms_norm(x, g_mix[l]) * (1 + sc_m) + sh_m
        i = l // 2
        if l % 2 == 0:
            y = fourier_hyena_mixer(h, w_in_ab[i], w_out_ab[i], conv_w[i], conv_b[i],
                                    filt_w1[i], filt_b1[i], filt_f1[i], filt_w2[i], filt_b2[i],
                                    filt_f2[i], filt_w3[i], decay[i], hyena_bias[i])
        else:
            y = neighbourhood_attention(h @ w_qkv[i], rpb[i]) @ w_out_c[i]
        x = x + gt_m * y
        h = rms_norm(x, g_ffn[l]) * (1 + sc_f) + sh_f
        x = x + gt_f * moe_ffn(h, router_w[l], router_b[l], exp_w1[l], exp_w3[l], exp_w2[l],
                               sh_w1[l], sh_w3[l], sh_w2[l])
    return rms_norm(x, g_final)


def setup_inputs(seed: int = 0) -> dict:
    key = jax.random.key(seed)
    ks = jax.random.split(key, 36)
    f32 = jnp.float32
    d = D_MODEL
    ne, no = (DEPTH + 1) // 2, DEPTH // 2
    d_ab_in = D_FNET + 3 * D_HYENA
    d_ab_out = D_FNET + D_HYENA
    d_attn = N_HEADS * HEAD_DIM

    def nrm(k, shape, scale):
        return scale * jax.random.normal(k, shape, f32)

    decay0 = jnp.linspace(DECAY_SLOW, DECAY_FAST, D_HYENA, dtype=f32)
    return {
        'x_prompt': nrm(ks[0], (BATCH, SEQ, d), 1.0),
        'x_sample': nrm(ks[1], (DEC_BATCH, DEC_SEQ, d), 1.0),
        'c_prompt': nrm(ks[2], (BATCH, d), 1.0),
        'c_sample': nrm(ks[3], (DEC_BATCH, d), 1.0),
        'ada_w': nrm(ks[4], (DEPTH, d, 6 * d), 0.5 * d ** -0.5),
        'ada_b': nrm(ks[5], (DEPTH, 6 * d), 0.02),
        'g_mix': 1.0 + nrm(ks[6], (DEPTH, d), 0.05),
        'g_ffn': 1.0 + nrm(ks[7], (DEPTH, d), 0.05),
        'g_final': 1.0 + nrm(ks[8], (d,), 0.05),
        'w_in_ab': nrm(ks[9], (ne, d, d_ab_in), d ** -0.5),
        'w_out_ab': nrm(ks[10], (ne, d_ab_out, d), d_ab_out ** -0.5),
        'conv_w': nrm(ks[11], (ne, SHORT_CONV, 3 * D_HYENA), SHORT_CONV ** -0.5),
        'conv_b': nrm(ks[12], (ne, 3 * D_HYENA), 0.02),
        'filt_w1': nrm(ks[13], (ne, POS_EMB, FILTER_HIDDEN), POS_EMB ** -0.5),
        'filt_b1': nrm(ks[14], (ne, FILTER_HIDDEN), 0.1),
        'filt_f1': 1.0 + nrm(ks[15], (ne, FILTER_HIDDEN), 0.05),
        'filt_w2': nrm(ks[16], (ne, FILTER_HIDDEN, FILTER_HIDDEN), FILTER_HIDDEN ** -0.5),
        'filt_b2': nrm(ks[17], (ne, FILTER_HIDDEN), 0.1),
        'filt_f2': 1.0 + nrm(ks[18], (ne, FILTER_HIDDEN), 0.05),
        'filt_w3': nrm(ks[19], (ne, FILTER_HIDDEN, 2 * D_HYENA), FILTER_HIDDEN ** -0.5),
        'decay': decay0 * (1.0 + nrm(ks[20], (ne, 2, D_HYENA), 0.05)),
        'hyena_bias': nrm(ks[21], (ne, D_HYENA), 0.1),
        'w_qkv': nrm(ks[22], (no, d, 3 * d_attn), d ** -0.5),
        'w_out_c': nrm(ks[23], (no, d_attn, d), d_attn ** -0.5),
        'rpb': nrm(ks[24], (no, N_HEADS, 2 * WIN_ROWS_MAX - 1, 2 * WIN_COLS - 1), 0.1),
        'router_w': nrm(ks[25], (DEPTH, d, N_EXPERTS), d ** -0.5),
        'router_b': nrm(ks[26], (DEPTH, N_EXPERTS), 0.01),
        'exp_w1': nrm(ks[27], (DEPTH, N_EXPERTS, d, D_EXPERT), d ** -0.5),
        'exp_w3': nrm(ks[28], (DEPTH, N_EXPERTS, d, D_EXPERT), d ** -0.5),
        'exp_w2': nrm(ks[29], (DEPTH, N_EXPERTS, D_EXPERT, d), D_EXPERT ** -0.5),
        'sh_w1': nrm(ks[30], (DEPTH, d, D_SHARED), d ** -0.5),
        'sh_w3': nrm(ks[31], (DEPTH, d, D_SHARED), d ** -0.5),
        'sh_w2': nrm(ks[32], (DEPTH, D_SHARED, d), D_SHARED ** -0.5),
    }


def reference(x_prompt, x_sample, c_prompt, c_sample, ada_w, ada_b, g_mix, g_ffn, g_final,
              w_in_ab, w_out_ab, conv_w, conv_b, filt_w1, filt_b1, filt_f1,
              filt_w2, filt_b2, filt_f2, filt_w3, decay, hyena_bias,
              w_qkv, w_out_c, rpb,
              router_w, router_b, exp_w1, exp_w3, exp_w2, sh_w1, sh_w3, sh_w2):
    params = (ada_w, ada_b, g_mix, g_ffn, g_final,
              w_in_ab, w_out_ab, conv_w, conv_b, filt_w1, filt_b1, filt_f1,
              filt_w2, filt_b2, filt_f2, filt_w3, decay, hyena_bias,
              w_qkv, w_out_c, rpb,
              router_w, router_b, exp_w1, exp_w3, exp_w2, sh_w1, sh_w3, sh_w2)
    y_prompt = encoder_trunk(x_prompt, c_prompt, *params)
    y_sample = encoder_trunk(x_sample, c_sample, *params)
    return (y_prompt, y_sample)
```

```python
import functools
import math

import ml_dtypes
import numpy as np

import jax
import jax.numpy as jnp
from jax import lax
from jax.experimental import pallas as pl
from jax.experimental.pallas import tpu as pltpu

F32 = jnp.float32
BF16 = jnp.bfloat16
I32 = jnp.int32

D_MODEL = 1024
GRID_W = 64
D_FNET = 512
N_FNET_GROUPS = 4
D_GROUP = D_FNET // N_FNET_GROUPS
D_HYENA = 512
POS_BANDS = 16
N_HEADS = 16
HEAD_DIM = 64
WIN_ROWS = 8
WIN_COLS = 16
N_EXPERTS = 64
TOP_K = 8
N_GROUPS = 8
GROUP_SIZE = N_EXPERTS // N_GROUPS
TOPK_GROUPS = 4
D_EXPERT = 256
D_SHARED = 256
ROUTED_SCALE = 2.5
EXPERT_CHUNK = 256
EPS = 1e-6
NEG_BIG = -1e30

LANES = 128
SUBLANES = 8
TM = 512
TC = 128
NB = 8
VMEM_LIMIT = 56 * 1024 * 1024


def _cparams(*sem):
    return pltpu.CompilerParams(dimension_semantics=sem, vmem_limit_bytes=VMEM_LIMIT)


def _dot(a, b):
    return jnp.dot(a, b, preferred_element_type=F32)


def _dot_nt(a, b):
    return lax.dot_general(a, b, (((1,), (1,)), ((), ())), preferred_element_type=F32)


def _split(x):
    hi = x.astype(BF16)
    lo = (x - hi.astype(F32)).astype(BF16)
    return hi, lo


def _dot3_lc(mh, ml, x):
    xh, xl = _split(x)
    return _dot(mh, xh) + _dot(mh, xl) + _dot(ml, xh)


def _dot3_rc(x, mh, ml):
    xh, xl = _split(x)
    return _dot(xh, mh) + _dot(xl, mh) + _dot(xh, ml)


def _dot3(a, b):
    ah, al = _split(a)
    bh, bl = _split(b)
    return _dot(ah, bh) + _dot(al, bh) + _dot(ah, bl)


def _silu(x):
    return x * jax.nn.sigmoid(x)


def _lane_tile(t, width):
    return jnp.concatenate([t] * (width // LANES), axis=-1)


def _cmul(ar, ai, br, bi):
    return ar * br - ai * bi, ar * bi + ai * br


def _norm_mod(x, g, sc, sh):
    ms = jnp.mean(x * x, axis=-1, keepdims=True)
    return (x * lax.rsqrt(ms + EPS) * g) * (1.0 + sc) + sh


def _seq_of_tile(i, bounds):
    s = jnp.int32(0)
    for b in bounds[:-1]:
        s = s + (i >= b).astype(I32)
    return s


def _hilo(m):
    m = np.asarray(m, np.float64)
    hi = m.astype(np.float32).astype(ml_dtypes.bfloat16)
    lo = (m - hi.astype(np.float64)).astype(np.float32).astype(ml_dtypes.bfloat16)
    return jnp.asarray(hi), jnp.asarray(lo)


def _cos_sin(n):
    j = np.arange(n)
    ang = 2.0 * np.pi * ((j[:, None] * j[None, :]) % n) / n
    return np.cos(ang), np.sin(ang)


def _twiddle(n_slab, n_row, n_total):
    s = jnp.arange(n_slab, dtype=I32)[:, None]
    r = jnp.arange(n_row, dtype=I32)[None, :]
    ang = ((s * r) % n_total).astype(F32) * (2.0 * math.pi / n_total)
    shape = (n_slab, n_row, LANES)
    return (jnp.broadcast_to(jnp.cos(ang)[:, :, None], shape),
            jnp.broadcast_to(-jnp.sin(ang)[:, :, None], shape))


def _mod_kernel(c_ref, w_ref, b_ref, o_ref):
    o_ref[...] = _dot3(_silu(c_ref[...]), w_ref[...]) + b_ref[...]


def _modulation(c_pad, ada_w, ada_b):
    depth, d, n = ada_w.shape
    tn = 1536
    return pl.pallas_call(
        _mod_kernel,
        grid=(depth, n // tn),
        in_specs=[pl.BlockSpec((SUBLANES, d), lambda l, j: (0, 0)),
                  pl.BlockSpec((None, d, tn), lambda l, j: (l, 0, j)),
                  pl.BlockSpec((None, 1, tn), lambda l, j: (l, 0, j))],
        out_specs=pl.BlockSpec((None, SUBLANES, tn), lambda l, j: (l, 0, j)),
        out_shape=jax.ShapeDtypeStruct((depth, SUBLANES, n), F32),
        compiler_params=_cparams("parallel", "parallel"),
        name="adaln_mod",
    )(c_pad, ada_w, ada_b.reshape(depth, 1, n))


def _nmm_kernel(x_ref, g_ref, sc_ref, sh_ref, w_ref, *o_refs):
    h = _norm_mod(x_ref[...], g_ref[...], sc_ref[...], sh_ref[...])
    u = _dot(h.astype(BF16), w_ref[...])
    off = 0
    for o in o_refs:
        n = o.shape[-1]
        o[...] = u[:, off:off + n].astype(o.dtype)
        off += n


def _norm_mod_matmul(x, g, sc, sh, w_bf16, splits, out_dtypes, bounds, name):
    t, d = x.shape
    n = w_bf16.shape[1]
    seq = lambda i: (_seq_of_tile(i, bounds), 0, 0)
    return pl.pallas_call(
        _nmm_kernel,
        grid=(t // TM,),
        in_specs=[pl.BlockSpec((TM, d), lambda i: (i, 0)),
                  pl.BlockSpec((1, d), lambda i: (0, 0)),
                  pl.BlockSpec((None, 1, d), seq),
                  pl.BlockSpec((None, 1, d), seq),
                  pl.BlockSpec((d, n), lambda i: (0, 0))],
        out_specs=[pl.BlockSpec((TM, s), lambda i: (i, 0)) for s in splits],
        out_shape=[jax.ShapeDtypeStruct((t, s), dt) for s, dt in zip(splits, out_dtypes)],
        compiler_params=_cparams("parallel"),
        name=name,
    )(x, g, sc, sh, w_bf16)


def _hyena_pre_kernel(prev_ref, cur_ref, next_ref, cw_ref, cb_ref, z_ref, x0_ref, *, first_tiles, last_tiles):
    i = pl.program_id(0)
    cur = cur_ref[...]
    tm = cur.shape[0]
    is_first = functools.reduce(jnp.logical_or, [i == f for f in first_tiles])
    is_last = functools.reduce(jnp.logical_or, [i == f for f in last_tiles])
    prow = jnp.where(is_first, 0.0, prev_ref[SUBLANES - 1:SUBLANES, :])
    nrow = jnp.where(is_last, 0.0, next_ref[0:1, :])
    rid = lax.broadcasted_iota(I32, (tm, 1), 0)
    up = jnp.where(rid == 0, prow, pltpu.roll(cur, 1, 0))
    dn = jnp.where(rid == tm - 1, nrow, pltpu.roll(cur, tm - 1, 0))
    cw = cw_ref[...]
    uc = cb_ref[...] + (up * cw[0:1, :] + cur * cw[1:2, :] + dn * cw[2:3, :])
    x0_ref[...] = uc[:, :D_HYENA]
    z_ref[...] = uc[:, 2 * D_HYENA:] * uc[:, D_HYENA:2 * D_HYENA]


def _hyena_pre(u_h, conv_w, conv_b, bounds):
    t, c = u_h.shape
    nt = t // TM
    rb = TM // SUBLANES
    first_tiles = (0,) + tuple(bounds[:-1])
    last_tiles = tuple(b - 1 for b in bounds)
    kern = functools.partial(_hyena_pre_kernel, first_tiles=first_tiles, last_tiles=last_tiles)
    return pl.pallas_call(
        kern,
        grid=(nt,),
        in_specs=[pl.BlockSpec((SUBLANES, c), lambda i: (jnp.maximum(i * rb - 1, 0), 0)),
                  pl.BlockSpec((TM, c), lambda i: (i, 0)),
                  pl.BlockSpec((SUBLANES, c), lambda i: (jnp.minimum((i + 1) * rb, nt * rb - 1), 0)),
                  pl.BlockSpec((3, c), lambda i: (0, 0)),
                  pl.BlockSpec((1, c), lambda i: (0, 0))],
        out_specs=[pl.BlockSpec((TM, D_HYENA), lambda i: (i, 0)),
                   pl.BlockSpec((TM, D_HYENA), lambda i: (i, 0))],
        out_shape=[jax.ShapeDtypeStruct((t, D_HYENA), F32), jax.ShapeDtypeStruct((t, D_HYENA), F32)],
        compiler_params=_cparams("parallel"),
        name="hyena_pre",
    )(u_h, u_h, u_h, conv_w, conv_b.reshape(1, c))


def _filter_kernel(bands_ref, w1_ref, b1_ref, f1_ref, w2_ref, b2_ref, f2_ref, w3_ref, dec_ref,
                   hf_ref, hb_ref, ss_ref, *, seq_len):
    i = pl.program_id(0)
    tm = hf_ref.shape[0]
    n = (i * tm + lax.broadcasted_iota(I32, (tm, 1), 0)).astype(F32)
    t = n / float(seq_len - 1)
    ang = (2.0 * math.pi / seq_len) * n * bands_ref[...]
    w1 = w1_ref[...]
    pre = (t * w1[0:1, :] + _dot3(jnp.cos(ang), w1[1:1 + POS_BANDS, :])
           + _dot3(-jnp.sin(ang), w1[1 + POS_BANDS:, :]) + b1_ref[...])
    h = jnp.sin(f1_ref[...] * pre)
    h = jnp.sin(f2_ref[...] * (_dot3(h, w2_ref[...]) + b2_ref[...]))
    h = _dot3(h, w3_ref[...])
    dec = jnp.abs(dec_ref[...])
    hf = h[:, :D_HYENA] * jnp.exp(-t * dec[0:1, :])
    hb = h[:, D_HYENA:] * jnp.exp(-t * dec[1:2, :])
    hb = jnp.where(n == 0.0, 0.0, hb)
    hf_ref[...] = hf
    hb_ref[...] = hb

    @pl.when(i == 0)
    def _():
        ss_ref[...] = jnp.zeros_like(ss_ref)

    ss_ref[...] += jnp.sum(hf * hf + hb * hb, axis=0, keepdims=True)


def _hyena_filter(seq_len, bands, w1, b1, f1, w2, b2, f2, w3, decay):
    tm = 512
    full = lambda a: pl.BlockSpec(a.shape, lambda i: (0,) * a.ndim)
    args = (bands, w1, b1.reshape(1, -1), f1.reshape(1, -1), w2, b2.reshape(1, -1), f2.reshape(1, -1), w3, decay)
    return pl.pallas_call(
        functools.partial(_filter_kernel, seq_len=seq_len),
        grid=(seq_len // tm,),
        in_specs=[full(a) for a in args],
        out_specs=[pl.BlockSpec((tm, D_HYENA), lambda i: (i, 0)),
                   pl.BlockSpec((tm, D_HYENA), lambda i: (i, 0)),
                   pl.BlockSpec((1, D_HYENA), lambda i: (0, 0))],
        out_shape=[jax.ShapeDtypeStruct((seq_len, D_HYENA), F32),
                   jax.ShapeDtypeStruct((seq_len, D_HYENA), F32),
                   jax.ShapeDtypeStruct((1, D_HYENA), F32)],
        compiler_params=_cparams("arbitrary"),
        name="hyena_filter",
    )(*args)


def _strided_fwd_kernel(x_ref, mh_ref, ml_ref, or_ref, oi_ref):
    n1 = or_ref.shape[0]
    mh = mh_ref[...]
    ml = ml_ref[...]
    for j in range(NB):
        a = _dot3_lc(mh, ml, x_ref[:, j, :])
        or_ref[:, j, :] = a[:n1]
        oi_ref[:, j, :] = a[n1:]


def _strided_fwd(x3, row_block, rows_in, m, batch, name):
    _, n2, ch = x3.shape
    mh, ml = m
    n1 = mh.shape[0] // 2
    out = jax.ShapeDtypeStruct((batch, n1, n2, ch), F32)
    ospec = pl.BlockSpec((None, n1, NB, ch), lambda b, j: (b, 0, j, 0))
    return pl.pallas_call(
        _strided_fwd_kernel,
        grid=(batch, n2 // NB),
        in_specs=[pl.BlockSpec((rows_in, NB, ch), lambda b, j: (row_block + b, j, 0)),
                  pl.BlockSpec(mh.shape, lambda b, j: (0, 0)),
                  pl.BlockSpec(ml.shape, lambda b, j: (0, 0))],
        out_specs=[ospec, ospec],
        out_shape=[out, out],
        compiler_params=_cparams("parallel", "parallel"),
        name=name,
    )(x3, mh, ml)


def _slab_specs(n1, n2, ch, batched):
    if batched:
        return pl.BlockSpec((None, NB, n2, ch), lambda b, k: (b, k, 0, 0))
    return pl.BlockSpec((None, NB, n2, ch), lambda b, k: (0, k, 0, 0))


def _slab_fwd_kernel(ar_ref, ai_ref, twr_ref, twi_ref, fh_ref, fl_ref, or_ref, oi_ref):
    n2, ch = ar_ref.shape[1:]
    fh = fh_ref[...]
    fl = fl_ref[...]
    for s in range(NB):
        twr = _lane_tile(twr_ref[s], ch)
        twi = _lane_tile(twi_ref[s], ch)
        br, bi = _cmul(ar_ref[s], ai_ref[s], twr, twi)
        x = _dot3_lc(fh, fl, jnp.concatenate([br, bi], axis=0))
        or_ref[s] = x[:n2]
        oi_ref[s] = x[n2:]


def _slab_conv_kernel(ar_ref, ai_ref, hr_ref, hi_ref, ss_ref, twr_ref, twi_ref,
                      fh_ref, fl_ref, gh_ref, gl_ref, or_ref, oi_ref):
    n2, ch = ar_ref.shape[1:]
    fh = fh_ref[...]
    fl = fl_ref[...]
    gh = gh_ref[...]
    gl = gl_ref[...]
    scale = lax.rsqrt(ss_ref[...] + EPS)
    for s in range(NB):
        twr = _lane_tile(twr_ref[s], ch)
        twi = _lane_tile(twi_ref[s], ch)
        br, bi = _cmul(ar_ref[s], ai_ref[s], twr, twi)
        x = _dot3_lc(fh, fl, jnp.concatenate([br, bi], axis=0))
        pr, pi = _cmul(x[:n2], x[n2:], hr_ref[s] * scale, hi_ref[s] * scale)
        g = _dot3_lc(gh, gl, jnp.concatenate([pr, pi], axis=0))
        qr, qi = _cmul(g[:n2], g[n2:], twr, -twi)
        or_ref[s] = qr
        oi_ref[s] = qi


def _hyena_out_kernel(gr_ref, gi_ref, z_ref, x0_ref, hb_ref, mh_ref, ml_ref, o_ref):
    mh = mh_ref[...]
    ml = ml_ref[...]
    hb = hb_ref[...]
    for j in range(NB):
        g = jnp.concatenate([gr_ref[:, j, :], gi_ref[:, j, :]], axis=0)
        zc = _dot3_lc(mh, ml, g)
        o_ref[:, j, :] = x0_ref[:, j, :] * (zc + z_ref[:, j, :] * hb)


class _HyenaPlan:
    def __init__(self, seq_len):
        n = 2 * seq_len
        self.n = n
        self.n1 = 256 if n >= 32768 else 128
        self.n2 = n // self.n1
        self.half = self.n1 // 2
        c1, s1 = _cos_sin(self.n1)
        c2, s2 = _cos_sin(self.n2)
        h = self.half
        self.m1 = _hilo(np.concatenate([c1, -s1], axis=0))
        self.m1_half = _hilo(np.concatenate([c1[:, :h], -s1[:, :h]], axis=0))
        self.f2 = _hilo(np.block([[c2, s2], [-s2, c2]]))
        self.g2 = _hilo(np.block([[c2, -s2], [s2, c2]]) / n)
        self.m2 = _hilo(np.concatenate([c1[:h], -s1[:h]], axis=1))


def _hyena_spectrum(plan, hc):
    n1, n2 = plan.n1, plan.n2
    ch = hc.shape[1]
    ar, ai = _strided_fwd(hc.reshape(n1, n2, ch), 0, n1, plan.m1, 1, "hyena_filt_s1")
    twr, twi = _twiddle(n1, n2, plan.n)
    fh, fl = plan.f2
    spec = _slab_specs(n1, n2, ch, True)
    tspec = pl.BlockSpec((NB, n2, LANES), lambda b, k: (k, 0, 0))
    cspec = pl.BlockSpec(fh.shape, lambda b, k: (0, 0))
    out = jax.ShapeDtypeStruct((1, n1, n2, ch), F32)
    return pl.pallas_call(
        _slab_fwd_kernel, grid=(1, n1 // NB),
        in_specs=[spec, spec, tspec, tspec, cspec, cspec],
        out_specs=[spec, spec], out_shape=[out, out],
        compiler_params=_cparams("parallel", "parallel"), name="hyena_filt_slab",
    )(ar, ai, twr, twi, fh, fl)


def _hyena_longconv(plan, z, x0, hyena_bias, hr, hi, ss, tok_off, batch, seq_len):
    n1, n2, half = plan.n1, plan.n2, plan.half
    t, ch = z.shape
    assert tok_off % seq_len == 0 and seq_len == half * n2
    row_block = tok_off // seq_len
    z3 = z.reshape(t // n2, n2, ch)
    x03 = x0.reshape(t // n2, n2, ch)
    ar, ai = _strided_fwd(z3, row_block, half, plan.m1_half, batch, "hyena_s1")
    twr, twi = _twiddle(n1, n2, plan.n)
    fh, fl = plan.f2
    gh, gl = plan.g2
    spec = _slab_specs(n1, n2, ch, True)
    hspec = _slab_specs(n1, n2, ch, False)
    tspec = pl.BlockSpec((NB, n2, LANES), lambda b, k: (k, 0, 0))
    cspec = pl.BlockSpec(fh.shape, lambda b, k: (0, 0))
    out = jax.ShapeDtypeStruct((batch, n1, n2, ch), F32)
    gr, gi = pl.pallas_call(
        _slab_conv_kernel, grid=(batch, n1 // NB),
        in_specs=[spec, spec, hspec, hspec, pl.BlockSpec((1, ch), lambda b, k: (0, 0)),
                  tspec, tspec, cspec, cspec, cspec, cspec],
        out_specs=[spec, spec], out_shape=[out, out],
        compiler_params=_cparams("parallel", "parallel"), name="hyena_slab_conv",
    )(ar, ai, hr, hi, ss, twr, twi, fh, fl, gh, gl)
    mh, ml = plan.m2
    gspec = pl.BlockSpec((None, n1, NB, ch), lambda b, j: (b, 0, j, 0))
    xspec = pl.BlockSpec((half, NB, ch), lambda b, j: (row_block + b, j, 0))
    y = pl.pallas_call(
        _hyena_out_kernel, grid=(batch, n2 // NB),
        in_specs=[gspec, gspec, xspec, xspec, pl.BlockSpec((1, ch), lambda b, j: (0, 0)),
                  pl.BlockSpec(mh.shape, lambda b, j: (0, 0)), pl.BlockSpec(ml.shape, lambda b, j: (0, 0))],
        out_specs=pl.BlockSpec((half, NB, ch), lambda b, j: (b, j, 0)),
        out_shape=jax.ShapeDtypeStruct((batch * half, n2, ch), F32),
        compiler_params=_cparams("parallel", "parallel"), name="hyena_s2",
    )(gr, gi, z3, x03, hyena_bias.reshape(1, ch), mh, ml)
    return y.reshape(batch * seq_len, ch)


def _fnet_s1_kernel(x_ref, wh_ref, wl_ref, mh_ref, ml_ref, or_ref, oi_ref):
    n1 = or_ref.shape[0]
    wh = wh_ref[...]
    wl = wl_ref[...]
    mh = mh_ref[...]
    ml = ml_ref[...]
    for j in range(NB):
        x = x_ref[:, j, :]
        zr, zi = [], []
        for g in range(N_FNET_GROUPS):
            zg = _dot3_rc(x[:, g * D_GROUP:(g + 1) * D_GROUP], wh, wl)
            zr.append(zg[:, :D_GROUP])
            zi.append(zg[:, D_GROUP:])
        z = jnp.concatenate([jnp.concatenate(zr, axis=1), jnp.concatenate(zi, axis=1)], axis=0)
        a = _dot3_lc(mh, ml, z)
        or_ref[:, j, :] = a[:n1]
        oi_ref[:, j, :] = a[n1:]


def _fnet_slab_kernel(ar_ref, ai_ref, twr_ref, twi_ref, fh_ref, fl_ref, o_ref, *, scale):
    ch = ar_ref.shape[2]
    fh = fh_ref[...]
    fl = fl_ref[...]
    for s in range(NB):
        twr = _lane_tile(twr_ref[s], ch)
        twi = _lane_tile(twi_ref[s], ch)
        br, bi = _cmul(ar_ref[s], ai_ref[s], twr, twi)
        o_ref[:, s, :] = _dot3_lc(fh, fl, jnp.concatenate([br, bi], axis=0)) * scale


def _fnet_mix(u_f, tok_off, batch, seq_len):
    t, ch = u_f.shape
    n2 = 128
    n1 = seq_len // n2
    assert tok_off % seq_len == 0
    row_block = tok_off // seq_len
    c1, s1 = _cos_sin(n1)
    c2, s2 = _cos_sin(n2)
    cg, sg = _cos_sin(D_GROUP)
    wh, wl = _hilo(np.concatenate([cg, -sg], axis=1))
    mh, ml = _hilo(np.block([[c1, s1], [-s1, c1]]))
    fh, fl = _hilo(np.concatenate([c2, s2], axis=1))
    u3 = u_f.reshape(t // n2, n2, ch)
    out = jax.ShapeDtypeStruct((batch, n1, n2, ch), F32)
    ospec = pl.BlockSpec((None, n1, NB, ch), lambda b, j: (b, 0, j, 0))
    c2d = lambda a: pl.BlockSpec(a.shape, lambda b, j: (0, 0))
    ar, ai = pl.pallas_call(
        _fnet_s1_kernel, grid=(batch, n2 // NB),
        in_specs=[pl.BlockSpec((n1, NB, ch), lambda b, j: (row_block + b, j, 0)),
                  c2d(wh), c2d(wl), c2d(mh), c2d(ml)],
        out_specs=[ospec, ospec], out_shape=[out, out],
        compiler_params=_cparams("parallel", "parallel"), name="fnet_s1",
    )(u3, wh, wl, mh, ml)
    twr, twi = _twiddle(n1, n2, seq_len)
    spec = _slab_specs(n1, n2, ch, True)
    tspec = pl.BlockSpec((NB, n2, LANES), lambda b, k: (k, 0, 0))
    scale = 1.0 / math.sqrt(seq_len * D_GROUP)
    y = pl.pallas_call(
        functools.partial(_fnet_slab_kernel, scale=scale), grid=(batch, n1 // NB),
        in_specs=[spec, spec, tspec, tspec, c2d(fh), c2d(fl)],
        out_specs=pl.BlockSpec((None, n2, NB, ch), lambda b, k: (b, 0, k, 0)),
        out_shape=jax.ShapeDtypeStruct((batch, n2, n1, ch), F32),
        compiler_params=_cparams("parallel", "parallel"), name="fnet_slab",
    )(ar, ai, twr, twi, fh, fl)
    return y.reshape(batch * seq_len, ch)


def _attn_kernel(q_ref, k_ref, v_ref, b_ref, o_ref):
    scale = HEAD_DIM ** -0.5
    outs = []
    for h in range(N_HEADS):
        sl = slice(h * HEAD_DIM, (h + 1) * HEAD_DIM)
        s = _dot_nt(q_ref[:, sl], k_ref[:, sl]) * scale + b_ref[h]
        m = jnp.max(s, axis=-1, keepdims=True)
        p = jnp.exp(s - m)
        l = jnp.sum(p, axis=-1, keepdims=True)
        outs.append(_dot(p.astype(BF16), v_ref[:, sl]) / l)
    o_ref[...] = jnp.concatenate(outs, axis=-1).astype(o_ref.dtype)


def _attn_bias_table(rpb):
    cols = jnp.arange(GRID_W)
    col_start = jnp.clip(cols - WIN_COLS // 2, 0, GRID_W - WIN_COLS)
    kc = jnp.arange(GRID_W)
    inside = (kc[None, :] >= col_start[:, None]) & (kc[None, :] < col_start[:, None] + WIN_COLS)
    cidx = jnp.clip(kc[None, :] - cols[:, None] + (WIN_COLS - 1), 0, 2 * WIN_COLS - 2)
    dd = jnp.arange(WIN_ROWS)
    ridx = (WIN_ROWS - 1) - dd[:, None] + jnp.arange(WIN_ROWS)[None, :]
    tab = rpb.astype(F32)[:, ridx][:, :, :, cidx]
    tab = jnp.where(inside[None, None, None], tab, NEG_BIG)
    tab = tab.transpose(1, 0, 3, 2, 4)
    return tab.reshape(WIN_ROWS, N_HEADS, GRID_W, WIN_ROWS * GRID_W)


def _attention(q, k, v, bias_tab, tok_off, batch, seq_len):
    rows = seq_len // GRID_W
    assert rows >= WIN_ROWS
    d = q.shape[1]
    kw = WIN_ROWS * GRID_W

    def rs_of(r):
        return jnp.clip(r - WIN_ROWS // 2, 0, rows - WIN_ROWS)

    kspec = pl.BlockSpec((pl.Element(kw), pl.Element(d)),
                         lambda b, r: (pl.multiple_of(tok_off + b * seq_len + rs_of(r) * GRID_W, GRID_W), 0))
    return pl.pallas_call(
        _attn_kernel, grid=(batch, rows),
        in_specs=[pl.BlockSpec((GRID_W, d), lambda b, r: (tok_off // GRID_W + b * rows + r, 0)),
                  kspec, kspec,
                  pl.BlockSpec((None, N_HEADS, GRID_W, kw), lambda b, r: (r - rs_of(r), 0, 0, 0))],
        out_specs=pl.BlockSpec((GRID_W, d), lambda b, r: (b * rows + r, 0)),
        out_shape=jax.ShapeDtypeStruct((batch * seq_len, d), BF16),
        compiler_params=_cparams("parallel", "parallel"), name="natten",
    )(q, k, v, bias_tab)


def _route(scores, rb):
    sel = scores + rb
    tm = sel.shape[1]
    sub = lax.broadcasted_iota(I32, (GROUP_SIZE, tm), 0)
    ninf = -jnp.inf
    groups = []
    for g in range(N_GROUPS):
        sg = sel[g * GROUP_SIZE:(g + 1) * GROUP_SIZE, :]
        m1 = jnp.max(sg, axis=0, keepdims=True)
        i1 = jnp.min(jnp.where(sg == m1, sub, GROUP_SIZE), axis=0, keepdims=True)
        m2 = jnp.max(jnp.where(sub == i1, ninf, sg), axis=0, keepdims=True)
        groups.append(m1 + m2)
    gs = jnp.concatenate(groups, axis=0)
    gsel = jnp.zeros((N_GROUPS, tm), F32)
    for _ in range(TOPK_GROUPS):
        m = jnp.max(gs, axis=0, keepdims=True)
        gi = jnp.min(jnp.where(gs == m, sub, N_GROUPS), axis=0, keepdims=True)
        hit = sub == gi
        gsel = jnp.where(hit, 1.0, gsel)
        gs = jnp.where(hit, ninf, gs)
    masked = jnp.concatenate(
        [jnp.where(gsel[g:g + 1, :] > 0.0, sel[g * GROUP_SIZE:(g + 1) * GROUP_SIZE, :], ninf)
         for g in range(N_GROUPS)], axis=0)
    row = lax.broadcasted_iota(I32, (N_EXPERTS, tm), 0)
    ids, ws = [], []
    for _ in range(TOP_K):
        m = jnp.max(masked, axis=0, keepdims=True)
        ii = jnp.min(jnp.where(masked == m, row, N_EXPERTS), axis=0, keepdims=True)
        hit = row == ii
        ws.append(jnp.sum(jnp.where(hit, scores, 0.0), axis=0, keepdims=True))
        ids.append(ii)
        masked = jnp.where(hit, ninf, masked)
    w = jnp.concatenate(ws, axis=0)
    w = w / jnp.sum(w, axis=0, keepdims=True) * ROUTED_SCALE
    return jnp.concatenate(ids, axis=0), w


def _post_mixer_kernel(*refs, n_parts, split_tile):
    i = pl.program_id(0)
    ys = refs[:2 * n_parts]
    ws = refs[2 * n_parts:3 * n_parts]
    (x_ref, gt_ref, g_ref, sc_ref, sh_ref, gtf_ref, rw_ref, rb_ref, s13_ref, s2_ref,
     xs_ref, h_ref, idx_ref, wts_ref) = refs[3 * n_parts:]
    acc = None
    for p in range(n_parts):
        y = jnp.where(i < split_tile, ys[2 * p][...], ys[2 * p + 1][...])
        d = _dot(y.astype(BF16), ws[p][...])
        acc = d if acc is None else acc + d
    x = x_ref[...] + gt_ref[...] * acc
    h = _norm_mod(x, g_ref[...], sc_ref[...], sh_ref[...])
    h_ref[...] = h
    hh, hl = _split(h)
    rh, rl = _split(rw_ref[...])
    logits = _dot_nt(rh, hh) + _dot_nt(rl, hh) + _dot_nt(rh, hl)
    ids, w = _route(jax.nn.sigmoid(logits), rb_ref[...])
    idx_ref[...] = ids
    wts_ref[...] = w
    u = _dot(hh, s13_ref[...])
    hs = _silu(u[:, :D_SHARED]) * u[:, D_SHARED:]
    xs_ref[...] = x + gtf_ref[...] * _dot(hs.astype(BF16), s2_ref[...])


def _post_mixer(parts, weights, x, gt, g, sc, sh, gtf, rw_t, rb, s13, s2, bounds, split_tile, name):
    t, d = x.shape
    nt = t // TM
    seq = lambda i: (_seq_of_tile(i, bounds), 0, 0)
    in_specs, args = [], []
    for yp, ys in parts:
        w = yp.shape[1]
        in_specs += [pl.BlockSpec((TM, w), lambda i: (jnp.minimum(i, split_tile - 1), 0)),
                     pl.BlockSpec((TM, w), lambda i: (jnp.maximum(i - split_tile, 0), 0))]
        args += [yp, ys]
    for w in weights:
        in_specs.append(pl.BlockSpec(w.shape, lambda i: (0, 0)))
        args.append(w)
    modspec = pl.BlockSpec((None, 1, d), seq)
    full = lambda a: pl.BlockSpec(a.shape, lambda i: (0, 0))
    in_specs += [pl.BlockSpec((TM, d), lambda i: (i, 0)), modspec, full(g), modspec, modspec, modspec,
                 full(rw_t), full(rb), full(s13), full(s2)]
    args += [x, gt, g, sc, sh, gtf, rw_t, rb, s13, s2]
    return pl.pallas_call(
        functools.partial(_post_mixer_kernel, n_parts=len(parts), split_tile=split_tile),
        grid=(nt,),
        in_specs=in_specs,
        out_specs=[pl.BlockSpec((TM, d), lambda i: (i, 0)),
                   pl.BlockSpec((TM, d), lambda i: (i, 0)),
                   pl.BlockSpec((TOP_K, TM), lambda i: (0, i)),
                   pl.BlockSpec((TOP_K, TM), lambda i: (0, i))],
        out_shape=[jax.ShapeDtypeStruct((t, d), F32), jax.ShapeDtypeStruct((t, d), F32),
                   jax.ShapeDtypeStruct((TOP_K, t), I32), jax.ShapeDtypeStruct((TOP_K, t), F32)],
        compiler_params=_cparams("parallel"), name=name,
    )(*args)


def _dispatch_plan(idx, wts):
    k, t = idx.shape
    tk = k * t
    e_flat = idx.reshape(tk)
    tok_flat = jnp.tile(jnp.arange(t, dtype=I32), k)
    order = jnp.argsort(e_flat, stable=True)
    e_s = e_flat[order]
    counts = jnp.bincount(e_flat, length=N_EXPERTS).astype(I32)
    starts = jnp.cumsum(counts) - counts
    padded = (counts + EXPERT_CHUNK - 1) // EXPERT_CHUNK * EXPERT_CHUNK
    pends = jnp.cumsum(padded)
    pstarts = pends - padded
    dest = pstarts[e_s] + jnp.arange(tk, dtype=I32) - starts[e_s]
    n_slots = tk + N_EXPERTS * EXPERT_CHUNK
    n_chunks = n_slots // EXPERT_CHUNK
    slot_tok = jnp.zeros((n_slots,), I32).at[dest].set(tok_flat[order])
    slot_w = jnp.zeros((n_slots,), F32).at[dest].set(wts.reshape(tk)[order])
    chunk_e = jnp.minimum(
        jnp.searchsorted(pends, jnp.arange(n_chunks, dtype=I32) * EXPERT_CHUNK, side='right'),
        N_EXPERTS - 1).astype(I32)
    slot_of = jnp.zeros((tk,), I32).at[order].set(dest).reshape(k, t)
    n_used = (pends[-1] // EXPERT_CHUNK).astype(I32).reshape(1)
    return (slot_tok.reshape(n_chunks, 1, EXPERT_CHUNK), slot_w.reshape(n_chunks, 1, EXPERT_CHUNK),
            chunk_e, n_used, slot_of)


def _expert_kernel(ce_ref, nu_ref, tokc_ref, tokn_ref, sw_ref, h_hbm, w1_ref, w3_ref, w2_ref,
                   o_ref, buf, sem):
    c = pl.program_id(0)
    n_used = nu_ref[0]
    slot = c % 2

    def row_copy(tok_ref, r, s):
        return pltpu.make_async_copy(h_hbm.at[pl.ds(tok_ref[0, 0, r], 1), :],
                                     buf.at[s, pl.ds(r, 1), :], sem.at[s])

    def issue(tok_ref, s):
        def body(r, carry):
            row_copy(tok_ref, r, s).start()
            return carry
        lax.fori_loop(0, EXPERT_CHUNK, body, 0, unroll=8)

    @pl.when(c == 0)
    def _():
        issue(tokc_ref, 0)

    @pl.when(c + 1 < n_used)
    def _():
        issue(tokn_ref, 1 - slot)

    @pl.when(c < n_used)
    def _():
        pltpu.make_async_copy(h_hbm.at[pl.ds(0, EXPERT_CHUNK), :], buf.at[slot], sem.at[slot]).wait()
        x = buf[slot].astype(BF16)
        a = _dot(x, w1_ref[...])
        b = _dot(x, w3_ref[...])
        y = _dot((_silu(a) * b).astype(BF16), w2_ref[...])
        wcol = jnp.transpose(jnp.broadcast_to(sw_ref[...], (LANES, EXPERT_CHUNK)))
        o_ref[...] = y * _lane_tile(wcol, y.shape[1])

    @pl.when(c >= n_used)
    def _():
        o_ref[...] = jnp.zeros_like(o_ref)


def _expert_ffn(h, slot_tok, slot_w, chunk_e, n_used, w1, w3, w2):
    n_chunks = slot_tok.shape[0]
    d = h.shape[1]
    grid_spec = pltpu.PrefetchScalarGridSpec(
        num_scalar_prefetch=2,
        grid=(n_chunks,),
        in_specs=[
            pl.BlockSpec((1, 1, EXPERT_CHUNK), lambda c, ce, nu: (c, 0, 0), memory_space=pltpu.SMEM),
            pl.BlockSpec((1, 1, EXPERT_CHUNK), lambda c, ce, nu: (jnp.minimum(c + 1, n_chunks - 1), 0, 0),
                         memory_space=pltpu.SMEM),
            pl.BlockSpec((None, 1, EXPERT_CHUNK), lambda c, ce, nu: (c, 0, 0)),
            pl.BlockSpec(memory_space=pl.ANY),
            pl.BlockSpec((None, d, D_EXPERT), lambda c, ce, nu: (ce[c], 0, 0)),
            pl.BlockSpec((None, d, D_EXPERT), lambda c, ce, nu: (ce[c], 0, 0)),
            pl.BlockSpec((None, D_EXPERT, d), lambda c, ce, nu: (ce[c], 0, 0)),
        ],
        out_specs=pl.BlockSpec((EXPERT_CHUNK, d), lambda c, ce, nu: (c, 0)),
        scratch_shapes=[pltpu.VMEM((2, EXPERT_CHUNK, d), F32), pltpu.SemaphoreType.DMA((2,))],
    )
    return pl.pallas_call(
        _expert_kernel, grid_spec=grid_spec,
        out_shape=jax.ShapeDtypeStruct((n_chunks * EXPERT_CHUNK, d), F32),
        compiler_params=_cparams("arbitrary"), name="moe_experts",
    )(chunk_e, n_used, slot_tok, slot_tok, slot_w, h, w1, w3, w2)


def _combine_kernel(sc_ref, sn_ref, xs_ref, gt_ref, ys_hbm, o_ref, buf, sem):
    i = pl.program_id(0)
    n = pl.num_programs(0)
    slot = i % 2

    def issue(s_ref, s):
        for k in range(TOP_K):
            def body(r, carry):
                pltpu.make_async_copy(ys_hbm.at[pl.ds(s_ref[k, r], 1), :],
                                      buf.at[s, k, pl.ds(r, 1), :], sem.at[s]).start()
                return carry
            lax.fori_loop(0, TC, body, 0, unroll=8)

    @pl.when(i == 0)
    def _():
        issue(sc_ref, 0)

    @pl.when(i + 1 < n)
    def _():
        issue(sn_ref, 1 - slot)

    for k in range(TOP_K):
        pltpu.make_async_copy(ys_hbm.at[pl.ds(0, TC), :], buf.at[slot, k], sem.at[slot]).wait()
    acc = buf[slot, 0]
    for k in range(1, TOP_K):
        acc = acc + buf[slot, k]
    o_ref[...] = xs_ref[...] + gt_ref[...] * acc


def _combine(slot_of, xs, gtf, ys, bounds_tc):
    t, d = xs.shape
    nt = t // TC
    seq = lambda i: (_seq_of_tile(i, bounds_tc), 0, 0)
    return pl.pallas_call(
        _combine_kernel, grid=(nt,),
        in_specs=[pl.BlockSpec((TOP_K, TC), lambda i: (0, i), memory_space=pltpu.SMEM),
                  pl.BlockSpec((TOP_K, TC), lambda i: (0, jnp.minimum(i + 1, nt - 1)), memory_space=pltpu.SMEM),
                  pl.BlockSpec((TC, d), lambda i: (i, 0)),
                  pl.BlockSpec((None, 1, d), seq),
                  pl.BlockSpec(memory_space=pl.ANY)],
        out_specs=pl.BlockSpec((TC, d), lambda i: (i, 0)),
        out_shape=jax.ShapeDtypeStruct((t, d), F32),
        scratch_shapes=[pltpu.VMEM((2, TOP_K, TC, d), F32), pltpu.SemaphoreType.DMA((2,))],
        compiler_params=_cparams("arbitrary"), name="moe_combine",
    )(slot_of, slot_of, xs, gtf, ys)


def _final_norm_kernel(x_ref, g_ref, o_ref):
    x = x_ref[...]
    ms = jnp.mean(x * x, axis=-1, keepdims=True)
    o_ref[...] = x * lax.rsqrt(ms + EPS) * g_ref[...]


def _final_norm(x, g, tile_off, n_tiles):
    d = x.shape[1]
    return pl.pallas_call(
        _final_norm_kernel, grid=(n_tiles,),
        in_specs=[pl.BlockSpec((TM, d), lambda i: (i + tile_off, 0)), pl.BlockSpec((1, d), lambda i: (0, 0))],
        out_specs=pl.BlockSpec((TM, d), lambda i: (i, 0)),
        out_shape=jax.ShapeDtypeStruct((n_tiles * TM, d), F32),
        compiler_params=_cparams("parallel"), name="final_norm",
    )(x, g)


def kernel(x_prompt, x_sample, c_prompt, c_sample, ada_w, ada_b, g_mix, g_ffn, g_final,
           w_in_ab, w_out_ab, conv_w, conv_b, filt_w1, filt_b1, filt_f1, filt_w2, filt_b2, filt_f2,
           filt_w3, decay, hyena_bias, w_qkv, w_out_c, rpb,
           router_w, router_b, exp_w1, exp_w3, exp_w2, sh_w1, sh_w3, sh_w2):
    d = D_MODEL
    bp, lp, _ = x_prompt.shape
    bs, ls, _ = x_sample.shape
    trunks = ((0, bp, lp), (bp * lp, bs, ls))
    t_all = bp * lp + bs * ls
    n_seq = bp + bs
    depth = ada_w.shape[0]
    assert lp % TM == 0 and ls % TM == 0
    bounds, acc = [], 0
    for _, b, l in trunks:
        for _ in range(b):
            acc += l // TM
            bounds.append(acc)
    bounds = tuple(bounds)
    bounds_tc = tuple(b * (TM // TC) for b in bounds)
    split_tile = bp * lp // TM

    x = jnp.concatenate([x_prompt.reshape(bp * lp, d), x_sample.reshape(bs * ls, d)], axis=0)
    c_all = jnp.concatenate([c_prompt, c_sample, jnp.zeros((SUBLANES - n_seq, d), F32)], axis=0)
    mod = _modulation(c_all, ada_w, ada_b)[:, :n_seq]
    mod = mod.reshape(depth, n_seq, 6, 1, d)

    bands = jnp.linspace(1e-4, POS_BANDS - 1, POS_BANDS, dtype=F32).reshape(1, POS_BANDS)
    bias_tab = None

    for l in range(depth):
        sh_m, sc_m, gt_m, sh_f, sc_f, gt_f = (mod[l, :, j] for j in range(6))
        g_m = g_mix[l].reshape(1, d)
        i = l // 2
        if l % 2 == 0:
            u_f, u_h = _norm_mod_matmul(x, g_m, sc_m, sh_m, w_in_ab[i].astype(BF16),
                                        (D_FNET, 3 * D_HYENA), (F32, F32), bounds, "in_proj_ab")
            z, x0 = _hyena_pre(u_h, conv_w[i], conv_b[i], bounds)
            ya, yb = [], []
            for tok_off, b, sl in trunks:
                plan = _HyenaPlan(sl)
                hf, hb, ss = _hyena_filter(sl, bands, filt_w1[i], filt_b1[i], filt_f1[i], filt_w2[i],
                                           filt_b2[i], filt_f2[i], filt_w3[i], decay[i])
                hc = jnp.concatenate([hf, jnp.zeros((1, D_HYENA), F32), hb[1:][::-1]], axis=0)
                hr, hi = _hyena_spectrum(plan, hc)
                yb.append(_hyena_longconv(plan, z, x0, hyena_bias[i], hr, hi, ss, tok_off, b, sl))
                ya.append(_fnet_mix(u_f, tok_off, b, sl))
            parts = [tuple(ya), tuple(yb)]
            w_out = w_out_ab[i].astype(BF16)
            weights = [w_out[:D_FNET], w_out[D_FNET:]]
        else:
            q, k, v = _norm_mod_matmul(x, g_m, sc_m, sh_m, w_qkv[i].astype(BF16),
                                       (d, d, d), (BF16, BF16, BF16), bounds, "qkv_proj")
            bias_tab = _attn_bias_table(rpb[i])
            o = tuple(_attention(q, k, v, bias_tab, tok_off, b, sl) for tok_off, b, sl in trunks)
            parts = [o]
            weights = [w_out_c[i].astype(BF16)]
        s13 = jnp.concatenate([sh_w1[l], sh_w3[l]], axis=1).astype(BF16)
        xs, h, idx, wts = _post_mixer(
            parts, weights, x, gt_m, g_ffn[l].reshape(1, d), sc_f, sh_f, gt_f,
            router_w[l].T, router_b[l].reshape(N_EXPERTS, 1), s13, sh_w2[l].astype(BF16),
            bounds, split_tile, "post_mixer_%d" % l)
        slot_tok, slot_w, chunk_e, n_used, slot_of = _dispatch_plan(idx, wts)
        ys = _expert_ffn(h, slot_tok, slot_w, chunk_e, n_used,
                         exp_w1[l].astype(BF16), exp_w3[l].astype(BF16), exp_w2[l].astype(BF16))
        x = _combine(slot_of, xs, gt_f, ys, bounds_tc)

    g_fin = g_final.reshape(1, d)
    y_prompt = _final_norm(x, g_fin, 0, bp * lp // TM).reshape(bp, lp, d)
    y_sample = _final_norm(x, g_fin, bp * lp // TM, bs * ls // TM).reshape(bs, ls, d)
    return (y_prompt, y_sample)
```

```python
import functools
import math

import ml_dtypes
import numpy as np

import jax
import jax.numpy as jnp
from jax import lax
from jax.experimental import pallas as pl
from jax.experimental.pallas import tpu as pltpu

F32 = jnp.float32
BF16 = jnp.bfloat16
I32 = jnp.int32

D_MODEL = 1024
GRID_W = 64
D_FNET = 512
N_FNET_GROUPS = 4
D_GROUP = D_FNET // N_FNET_GROUPS
D_HYENA = 512
POS_BANDS = 16
N_HEADS = 16
HEAD_DIM = 64
WIN_ROWS = 8
WIN_COLS = 16
N_EXPERTS = 64
TOP_K = 8
N_GROUPS = 8
GROUP_SIZE = N_EXPERTS // N_GROUPS
TOPK_GROUPS = 4
D_EXPERT = 256
D_SHARED = 256
ROUTED_SCALE = 2.5
EXPERT_CHUNK = 256
EPS = 1e-6
NEG_BIG = -1e30

LANES = 128
SUBLANES = 8
TM = 512
TC = 128
NB = 8
VMEM_LIMIT = 56 * 1024 * 1024


def _cparams(*sem):
    return pltpu.CompilerParams(dimension_semantics=sem, vmem_limit_bytes=VMEM_LIMIT)


def _dot(a, b):
    return jnp.dot(a, b, preferred_element_type=F32)


def _dot_nt(a, b):
    return lax.dot_general(a, b, (((1,), (1,)), ((), ())), preferred_element_type=F32)


def _split(x):
    hi = x.astype(BF16)
    lo = (x - hi.astype(F32)).astype(BF16)
    return hi, lo


def _dot3_lc(mh, ml, x):
    xh, xl = _split(x)
    return _dot(mh, xh) + _dot(mh, xl) + _dot(ml, xh)


def _dot3_rc(x, mh, ml):
    xh, xl = _split(x)
    return _dot(xh, mh) + _dot(xl, mh) + _dot(xh, ml)


def _dot3(a, b):
    ah, al = _split(a)
    bh, bl = _split(b)
    return _dot(ah, bh) + _dot(al, bh) + _dot(ah, bl)


def _silu(x):
    return x * jax.nn.sigmoid(x)


def _lane_tile(t, width):
    return jnp.concatenate([t] * (width // LANES), axis=-1)


def _cmul(ar, ai, br, bi):
    return ar * br - ai * bi, ar * bi + ai * br


def _norm_mod(x, g, sc, sh):
    ms = jnp.mean(x * x, axis=-1, keepdims=True)
    return (x * lax.rsqrt(ms + EPS) * g) * (1.0 + sc) + sh


def _seq_of_tile(i, bounds):
    s = jnp.int32(0)
    for b in bounds[:-1]:
        s = s + (i >= b).astype(I32)
    return s


def _hilo(m):
    m = np.asarray(m, np.float64)
    hi = m.astype(np.float32).astype(ml_dtypes.bfloat16)
    lo = (m - hi.astype(np.float64)).astype(np.float32).astype(ml_dtypes.bfloat16)
    return jnp.asarray(hi), jnp.asarray(lo)


def _cos_sin(n):
    j = np.arange(n)
    ang = 2.0 * np.pi * ((j[:, None] * j[None, :]) % n) / n
    return np.cos(ang), np.sin(ang)


def _twiddle(n_slab, n_row, n_total):
    s = jnp.arange(n_slab, dtype=I32)[:, None]
    r = jnp.arange(n_row, dtype=I32)[None, :]
    ang = ((s * r) % n_total).astype(F32) * (2.0 * math.pi / n_total)
    shape = (n_slab, n_row, LANES)
    return (jnp.broadcast_to(jnp.cos(ang)[:, :, None], shape),
            jnp.broadcast_to(-jnp.sin(ang)[:, :, None], shape))


def _mod_kernel(c_ref, w_ref, b_ref, o_ref):
    o_ref[...] = _dot3(_silu(c_ref[...]), w_ref[...]) + b_ref[...]


def _modulation(c_pad, ada_w, ada_b):
    depth, d, n = ada_w.shape
    tn = 1536
    return pl.pallas_call(
        _mod_kernel,
        grid=(depth, n // tn),
        in_specs=[pl.BlockSpec((SUBLANES, d), lambda l, j: (0, 0)),
                  pl.BlockSpec((None, d, tn), lambda l, j: (l, 0, j)),
                  pl.BlockSpec((None, 1, tn), lambda l, j: (l, 0, j))],
        out_specs=pl.BlockSpec((None, SUBLANES, tn), lambda l, j: (l, 0, j)),
        out_shape=jax.ShapeDtypeStruct((depth, SUBLANES, n), F32),
        compiler_params=_cparams("parallel", "parallel"),
        name="adaln_mod",
    )(c_pad, ada_w, ada_b.reshape(depth, 1, n))


def _nmm_kernel(x_ref, g_ref, sc_ref, sh_ref, w_ref, *o_refs):
    h = _norm_mod(x_ref[...], g_ref[...], sc_ref[...], sh_ref[...])
    u = _dot(h.astype(BF16), w_ref[...])
    off = 0
    for o in o_refs:
        n = o.shape[-1]
        o[...] = u[:, off:off + n].astype(o.dtype)
        off += n


def _norm_mod_matmul(x, g, sc, sh, w_bf16, splits, out_dtypes, bounds, name):
    t, d = x.shape
    n = w_bf16.shape[1]
    seq = lambda i: (_seq_of_tile(i, bounds), 0, 0)
    return pl.pallas_call(
        _nmm_kernel,
        grid=(t // TM,),
        in_specs=[pl.BlockSpec((TM, d), lambda i: (i, 0)),
                  pl.BlockSpec((1, d), lambda i: (0, 0)),
                  pl.BlockSpec((None, 1, d), seq),
                  pl.BlockSpec((None, 1, d), seq),
                  pl.BlockSpec((d, n), lambda i: (0, 0))],
        out_specs=[pl.BlockSpec((TM, s), lambda i: (i, 0)) for s in splits],
        out_shape=[jax.ShapeDtypeStruct((t, s), dt) for s, dt in zip(splits, out_dtypes)],
        compiler_params=_cparams("parallel"),
        name=name,
    )(x, g, sc, sh, w_bf16)


def _hyena_pre_kernel(prev_ref, cur_ref, next_ref, cw_ref, cb_ref, z_ref, x0_ref, *, first_tiles, last_tiles):
    i = pl.program_id(0)
    cur = cur_ref[...]
    tm = cur.shape[0]
    is_first = functools.reduce(jnp.logical_or, [i == f for f in first_tiles])
    is_last = functools.reduce(jnp.logical_or, [i == f for f in last_tiles])
    prow = jnp.where(is_first, 0.0, prev_ref[SUBLANES - 1:SUBLANES, :])
    nrow = jnp.where(is_last, 0.0, next_ref[0:1, :])
    rid = lax.broadcasted_iota(I32, (tm, 1), 0)
    up = jnp.where(rid == 0, prow, pltpu.roll(cur, 1, 0))
    dn = jnp.where(rid == tm - 1, nrow, pltpu.roll(cur, tm - 1, 0))
    cw = cw_ref[...]
    uc = cb_ref[...] + (up * cw[0:1, :] + cur * cw[1:2, :] + dn * cw[2:3, :])
    x0_ref[...] = uc[:, :D_HYENA]
    z_ref[...] = uc[:, 2 * D_HYENA:] * uc[:, D_HYENA:2 * D_HYENA]


def _hyena_pre(u_h, conv_w, conv_b, bounds):
    t, c = u_h.shape
    nt = t // TM
    rb = TM // SUBLANES
    first_tiles = (0,) + tuple(bounds[:-1])
    last_tiles = tuple(b - 1 for b in bounds)
    kern = functools.partial(_hyena_pre_kernel, first_tiles=first_tiles, last_tiles=last_tiles)
    return pl.pallas_call(
        kern,
        grid=(nt,),
        in_specs=[pl.BlockSpec((SUBLANES, c), lambda i: (jnp.maximum(i * rb - 1, 0), 0)),
                  pl.BlockSpec((TM, c), lambda i: (i, 0)),
                  pl.BlockSpec((SUBLANES, c), lambda i: (jnp.minimum((i + 1) * rb, nt * rb - 1), 0)),
                  pl.BlockSpec((3, c), lambda i: (0, 0)),
                  pl.BlockSpec((1, c), lambda i: (0, 0))],
        out_specs=[pl.BlockSpec((TM, D_HYENA), lambda i: (i, 0)),
                   pl.BlockSpec((TM, D_HYENA), lambda i: (i, 0))],
        out_shape=[jax.ShapeDtypeStruct((t, D_HYENA), F32), jax.ShapeDtypeStruct((t, D_HYENA), F32)],
        compiler_params=_cparams("parallel"),
        name="hyena_pre",
    )(u_h, u_h, u_h, conv_w, conv_b.reshape(1, c))


def _filter_kernel(bands_ref, w1_ref, b1_ref, f1_ref, w2_ref, b2_ref, f2_ref, w3_ref, dec_ref,
                   hf_ref, hb_ref, ss_ref, *, seq_len):
    i = pl.program_id(0)
    tm = hf_ref.shape[0]
    n = (i * tm + lax.broadcasted_iota(I32, (tm, 1), 0)).astype(F32)
    t = n / float(seq_len - 1)
    ang = (2.0 * math.pi / seq_len) * n * bands_ref[...]
    w1 = w1_ref[...]
    pre = (t * w1[0:1, :] + _dot3(jnp.cos(ang), w1[1:1 + POS_BANDS, :])
           + _dot3(-jnp.sin(ang), w1[1 + POS_BANDS:, :]) + b1_ref[...])
    h = jnp.sin(f1_ref[...] * pre)
    h = jnp.sin(f2_ref[...] * (_dot3(h, w2_ref[...]) + b2_ref[...]))
    h = _dot3(h, w3_ref[...])
    dec = jnp.abs(dec_ref[...])
    hf = h[:, :D_HYENA] * jnp.exp(-t * dec[0:1, :])
    hb = h[:, D_HYENA:] * jnp.exp(-t * dec[1:2, :])
    hb = jnp.where(n == 0.0, 0.0, hb)
    hf_ref[...] = hf
    hb_ref[...] = hb

    @pl.when(i == 0)
    def _():
        ss_ref[...] = jnp.zeros_like(ss_ref)

    ss_ref[...] += jnp.sum(hf * hf + hb * hb, axis=0, keepdims=True)


def _hyena_filter(seq_len, bands, w1, b1, f1, w2, b2, f2, w3, decay):
    tm = 512
    full = lambda a: pl.BlockSpec(a.shape, lambda i: (0,) * a.ndim)
    args = (bands, w1, b1.reshape(1, -1), f1.reshape(1, -1), w2, b2.reshape(1, -1), f2.reshape(1, -1), w3, decay)
    return pl.pallas_call(
        functools.partial(_filter_kernel, seq_len=seq_len),
        grid=(seq_len // tm,),
        in_specs=[full(a) for a in args],
        out_specs=[pl.BlockSpec((tm, D_HYENA), lambda i: (i, 0)),
                   pl.BlockSpec((tm, D_HYENA), lambda i: (i, 0)),
                   pl.BlockSpec((1, D_HYENA), lambda i: (0, 0))],
        out_shape=[jax.ShapeDtypeStruct((seq_len, D_HYENA), F32),
                   jax.ShapeDtypeStruct((seq_len, D_HYENA), F32),
                   jax.ShapeDtypeStruct((1, D_HYENA), F32)],
        compiler_params=_cparams("arbitrary"),
        name="hyena_filter",
    )(*args)


def _strided_fwd_kernel(x_ref, mh_ref, ml_ref, or_ref, oi_ref):
    n1 = or_ref.shape[0]
    mh = mh_ref[...]
    ml = ml_ref[...]
    for j in range(NB):
        a = _dot3_lc(mh, ml, x_ref[:, j, :])
        or_ref[:, j, :] = a[:n1]
        oi_ref[:, j, :] = a[n1:]


def _strided_fwd(x3, row_block, rows_in, m, batch, name):
    _, n2, ch = x3.shape
    mh, ml = m
    n1 = mh.shape[0] // 2
    out = jax.ShapeDtypeStruct((batch, n1, n2, ch), F32)
    ospec = pl.BlockSpec((None, n1, NB, ch), lambda b, j: (b, 0, j, 0))
    return pl.pallas_call(
        _strided_fwd_kernel,
        grid=(batch, n2 // NB),
        in_specs=[pl.BlockSpec((rows_in, NB, ch), lambda b, j: (row_block + b, j, 0)),
                  pl.BlockSpec(mh.shape, lambda b, j: (0, 0)),
                  pl.BlockSpec(ml.shape, lambda b, j: (0, 0))],
        out_specs=[ospec, ospec],
        out_shape=[out, out],
        compiler_params=_cparams("parallel", "parallel"),
        name=name,
    )(x3, mh, ml)


def _slab_specs(n1, n2, ch, batched):
    if batched:
        return pl.BlockSpec((None, NB, n2, ch), lambda b, k: (b, k, 0, 0))
    return pl.BlockSpec((None, NB, n2, ch), lambda b, k: (0, k, 0, 0))


def _slab_fwd_kernel(ar_ref, ai_ref, twr_ref, twi_ref, fh_ref, fl_ref, or_ref, oi_ref):
    n2, ch = ar_ref.shape[1:]
    fh = fh_ref[...]
    fl = fl_ref[...]
    for s in range(NB):
        twr = _lane_tile(twr_ref[s], ch)
        twi = _lane_tile(twi_ref[s], ch)
        br, bi = _cmul(ar_ref[s], ai_ref[s], twr, twi)
        x = _dot3_lc(fh, fl, jnp.concatenate([br, bi], axis=0))
        or_ref[s] = x[:n2]
        oi_ref[s] = x[n2:]


def _slab_conv_kernel(ar_ref, ai_ref, hr_ref, hi_ref, ss_ref, twr_ref, twi_ref,
                      fh_ref, fl_ref, gh_ref, gl_ref, or_ref, oi_ref):
    n2, ch = ar_ref.shape[1:]
    fh = fh_ref[...]
    fl = fl_ref[...]
    gh = gh_ref[...]
    gl = gl_ref[...]
    scale = lax.rsqrt(ss_ref[...] + EPS)
    for s in range(NB):
        twr = _lane_tile(twr_ref[s], ch)
        twi = _lane_tile(twi_ref[s], ch)
        br, bi = _cmul(ar_ref[s], ai_ref[s], twr, twi)
        x = _dot3_lc(fh, fl, jnp.concatenate([br, bi], axis=0))
        pr, pi = _cmul(x[:n2], x[n2:], hr_ref[s] * scale, hi_ref[s] * scale)
        g = _dot3_lc(gh, gl, jnp.concatenate([pr, pi], axis=0))
        qr, qi = _cmul(g[:n2], g[n2:], twr, -twi)
        or_ref[s] = qr
        oi_ref[s] = qi


def _hyena_out_kernel(gr_ref, gi_ref, z_ref, x0_ref, hb_ref, mh_ref, ml_ref, o_ref):
    mh = mh_ref[...]
    ml = ml_ref[...]
    hb = hb_ref[...]
    for j in range(NB):
        g = jnp.concatenate([gr_ref[:, j, :], gi_ref[:, j, :]], axis=0)
        zc = _dot3_lc(mh, ml, g)
        o_ref[:, j, :] = x0_ref[:, j, :] * (zc + z_ref[:, j, :] * hb)


class _HyenaPlan:
    def __init__(self, seq_len):
        n = 2 * seq_len
        self.n = n
        self.n1 = 256 if n >= 32768 else 128
        self.n2 = n // self.n1
        self.half = self.n1 // 2
        c1, s1 = _cos_sin(self.n1)
        c2, s2 = _cos_sin(self.n2)
        h = self.half
        self.m1 = _hilo(np.concatenate([c1, -s1], axis=0))
        self.m1_half = _hilo(np.concatenate([c1[:, :h], -s1[:, :h]], axis=0))
        self.f2 = _hilo(np.block([[c2, s2], [-s2, c2]]))
        self.g2 = _hilo(np.block([[c2, -s2], [s2, c2]]) / n)
        self.m2 = _hilo(np.concatenate([c1[:h], -s1[:h]], axis=1))


def _hyena_spectrum(plan, hc):
    n1, n2 = plan.n1, plan.n2
    ch = hc.shape[1]
    ar, ai = _strided_fwd(hc.reshape(n1, n2, ch), 0, n1, plan.m1, 1, "hyena_filt_s1")
    twr, twi = _twiddle(n1, n2, plan.n)
    fh, fl = plan.f2
    spec = _slab_specs(n1, n2, ch, True)
    tspec = pl.BlockSpec((NB, n2, LANES), lambda b, k: (k, 0, 0))
    cspec = pl.BlockSpec(fh.shape, lambda b, k: (0, 0))
    out = jax.ShapeDtypeStruct((1, n1, n2, ch), F32)
    return pl.pallas_call(
        _slab_fwd_kernel, grid=(1, n1 // NB),
        in_specs=[spec, spec, tspec, tspec, cspec, cspec],
        out_specs=[spec, spec], out_shape=[out, out],
        compiler_params=_cparams("parallel", "parallel"), name="hyena_filt_slab",
    )(ar, ai, twr, twi, fh, fl)


def _hyena_longconv(plan, z, x0, hyena_bias, hr, hi, ss, tok_off, batch, seq_len):
    n1, n2, half = plan.n1, plan.n2, plan.half
    t, ch = z.shape
    assert tok_off % seq_len == 0 and seq_len == half * n2
    row_block = tok_off // seq_len
    z3 = z.reshape(t // n2, n2, ch)
    x03 = x0.reshape(t // n2, n2, ch)
    ar, ai = _strided_fwd(z3, row_block, half, plan.m1_half, batch, "hyena_s1")
    twr, twi = _twiddle(n1, n2, plan.n)
    fh, fl = plan.f2
    gh, gl = plan.g2
    spec = _slab_specs(n1, n2, ch, True)
    hspec = _slab_specs(n1, n2, ch, False)
    tspec = pl.BlockSpec((NB, n2, LANES), lambda b, k: (k, 0, 0))
    cspec = pl.BlockSpec(fh.shape, lambda b, k: (0, 0))
    out = jax.ShapeDtypeStruct((batch, n1, n2, ch), F32)
    gr, gi = pl.pallas_call(
        _slab_conv_kernel, grid=(batch, n1 // NB),
        in_specs=[spec, spec, hspec, hspec, pl.BlockSpec((1, ch), lambda b, k: (0, 0)),
                  tspec, tspec, cspec, cspec, cspec, cspec],
        out_specs=[spec, spec], out_shape=[out, out],
        compiler_params=_cparams("parallel", "parallel"), name="hyena_slab_conv",
    )(ar, ai, hr, hi, ss, twr, twi, fh, fl, gh, gl)
    mh, ml = plan.m2
    gspec = pl.BlockSpec((None, n1, NB, ch), lambda b, j: (b, 0, j, 0))
    xspec = pl.BlockSpec((half, NB, ch), lambda b, j: (row_block + b, j, 0))
    y = pl.pallas_call(
        _hyena_out_kernel, grid=(batch, n2 // NB),
        in_specs=[gspec, gspec, xspec, xspec, pl.BlockSpec((1, ch), lambda b, j: (0, 0)),
                  pl.BlockSpec(mh.shape, lambda b, j: (0, 0)), pl.BlockSpec(ml.shape, lambda b, j: (0, 0))],
        out_specs=pl.BlockSpec((half, NB, ch), lambda b, j: (b, j, 0)),
        out_shape=jax.ShapeDtypeStruct((batch * half, n2, ch), F32),
        compiler_params=_cparams("parallel", "parallel"), name="hyena_s2",
    )(gr, gi, z3, x03, hyena_bias.reshape(1, ch), mh, ml)
    return y.reshape(batch * seq_len, ch)


def _fnet_s1_kernel(x_ref, wh_ref, wl_ref, mh_ref, ml_ref, or_ref, oi_ref):
    n1 = or_ref.shape[0]
    wh = wh_ref[...]
    wl = wl_ref[...]
    mh = mh_ref[...]
    ml = ml_ref[...]
    for j in range(NB):
        x = x_ref[:, j, :]
        zr, zi = [], []
        for g in range(N_FNET_GROUPS):
            zg = _dot3_rc(x[:, g * D_GROUP:(g + 1) * D_GROUP], wh, wl)
            zr.append(zg[:, :D_GROUP])
            zi.append(zg[:, D_GROUP:])
        z = jnp.concatenate([jnp.concatenate(zr, axis=1), jnp.concatenate(zi, axis=1)], axis=0)
        a = _dot3_lc(mh, ml, z)
        or_ref[:, j, :] = a[:n1]
        oi_ref[:, j, :] = a[n1:]


def _fnet_slab_kernel(ar_ref, ai_ref, twr_ref, twi_ref, fh_ref, fl_ref, o_ref, *, scale):
    ch = ar_ref.shape[2]
    fh = fh_ref[...]
    fl = fl_ref[...]
    for s in range(NB):
        twr = _lane_tile(twr_ref[s], ch)
        twi = _lane_tile(twi_ref[s], ch)
        br, bi = _cmul(ar_ref[s], ai_ref[s], twr, twi)
        o_ref[:, s, :] = _dot3_lc(fh, fl, jnp.concatenate([br, bi], axis=0)) * scale


def _fnet_mix(u_f, tok_off, batch, seq_len):
    t, ch = u_f.shape
    n2 = 128
    n1 = seq_len // n2
    assert tok_off % seq_len == 0
    row_block = tok_off // seq_len
    c1, s1 = _cos_sin(n1)
    c2, s2 = _cos_sin(n2)
    cg, sg = _cos_sin(D_GROUP)
    wh, wl = _hilo(np.concatenate([cg, -sg], axis=1))
    mh, ml = _hilo(np.block([[c1, s1], [-s1, c1]]))
    fh, fl = _hilo(np.concatenate([c2, s2], axis=1))
    u3 = u_f.reshape(t // n2, n2, ch)
    out = jax.ShapeDtypeStruct((batch, n1, n2, ch), F32)
    ospec = pl.BlockSpec((None, n1, NB, ch), lambda b, j: (b, 0, j, 0))
    c2d = lambda a: pl.BlockSpec(a.shape, lambda b, j: (0, 0))
    ar, ai = pl.pallas_call(
        _fnet_s1_kernel, grid=(batch, n2 // NB),
        in_specs=[pl.BlockSpec((n1, NB, ch), lambda b, j: (row_block + b, j, 0)),
                  c2d(wh), c2d(wl), c2d(mh), c2d(ml)],
        out_specs=[ospec, ospec], out_shape=[out, out],
        compiler_params=_cparams("parallel", "parallel"), name="fnet_s1",
    )(u3, wh, wl, mh, ml)
    twr, twi = _twiddle(n1, n2, seq_len)
    spec = _slab_specs(n1, n2, ch, True)
    tspec = pl.BlockSpec((NB, n2, LANES), lambda b, k: (k, 0, 0))
    scale = 1.0 / math.sqrt(seq_len * D_GROUP)
    y = pl.pallas_call(
        functools.partial(_fnet_slab_kernel, scale=scale), grid=(batch, n1 // NB),
        in_specs=[spec, spec, tspec, tspec, c2d(fh), c2d(fl)],
        out_specs=pl.BlockSpec((None, n2, NB, ch), lambda b, k: (b, 0, k, 0)),
        out_shape=jax.ShapeDtypeStruct((batch, n2, n1, ch), F32),
        compiler_params=_cparams("parallel", "parallel"), name="fnet_slab",
    )(ar, ai, twr, twi, fh, fl)
    return y.reshape(batch * seq_len, ch)


def _attn_kernel(q_ref, k_ref, v_ref, b_ref, o_ref):
    scale = HEAD_DIM ** -0.5
    outs = []
    for h in range(N_HEADS):
        sl = slice(h * HEAD_DIM, (h + 1) * HEAD_DIM)
        s = _dot_nt(q_ref[:, sl], k_ref[:, sl]) * scale + b_ref[h]
        m = jnp.max(s, axis=-1, keepdims=True)
        p = jnp.exp(s - m)
        l = jnp.sum(p, axis=-1, keepdims=True)
        outs.append(_dot(p.astype(BF16), v_ref[:, sl]) / l)
    o_ref[...] = jnp.concatenate(outs, axis=-1).astype(o_ref.dtype)


def _attn_bias_table(rpb):
    cols = jnp.arange(GRID_W)
    col_start = jnp.clip(cols - WIN_COLS // 2, 0, GRID_W - WIN_COLS)
    kc = jnp.arange(GRID_W)
    inside = (kc[None, :] >= col_start[:, None]) & (kc[None, :] < col_start[:, None] + WIN_COLS)
    cidx = jnp.clip(kc[None, :] - cols[:, None] + (WIN_COLS - 1), 0, 2 * WIN_COLS - 2)
    dd = jnp.arange(WIN_ROWS)
    ridx = (WIN_ROWS - 1) - dd[:, None] + jnp.arange(WIN_ROWS)[None, :]
    tab = rpb.astype(F32)[:, ridx][:, :, :, cidx]
    tab = jnp.where(inside[None, None, None], tab, NEG_BIG)
    tab = tab.transpose(1, 0, 3, 2, 4)
    return tab.reshape(WIN_ROWS, N_HEADS, GRID_W, WIN_ROWS * GRID_W)


def _attention(q, k, v, bias_tab, tok_off, batch, seq_len):
    rows = seq_len // GRID_W
    assert rows >= WIN_ROWS
    d = q.shape[1]
    kw = WIN_ROWS * GRID_W

    def rs_of(r):
        return jnp.clip(r - WIN_ROWS // 2, 0, rows - WIN_ROWS)

    kspec = pl.BlockSpec((pl.Element(kw), pl.Element(d)),
                         lambda b, r: (pl.multiple_of(tok_off + b * seq_len + rs_of(r) * GRID_W, GRID_W), 0))
    return pl.pallas_call(
        _attn_kernel, grid=(batch, rows),
        in_specs=[pl.BlockSpec((GRID_W, d), lambda b, r: (tok_off // GRID_W + b * rows + r, 0)),
                  kspec, kspec,
                  pl.BlockSpec((None, N_HEADS, GRID_W, kw), lambda b, r: (r - rs_of(r), 0, 0, 0))],
        out_specs=pl.BlockSpec((GRID_W, d), lambda b, r: (b * rows + r, 0)),
        out_shape=jax.ShapeDtypeStruct((batch * seq_len, d), BF16),
        compiler_params=_cparams("parallel", "parallel"), name="natten",
    )(q, k, v, bias_tab)


def _route(scores, rb):
    sel = scores + rb
    tm = sel.shape[1]
    sub = lax.broadcasted_iota(I32, (GROUP_SIZE, tm), 0)
    ninf = -jnp.inf
    groups = []
    for g in range(N_GROUPS):
        sg = sel[g * GROUP_SIZE:(g + 1) * GROUP_SIZE, :]
        m1 = jnp.max(sg, axis=0, keepdims=True)
        i1 = jnp.min(jnp.where(sg == m1, sub, GROUP_SIZE), axis=0, keepdims=True)
        m2 = jnp.max(jnp.where(sub == i1, ninf, sg), axis=0, keepdims=True)
        groups.append(m1 + m2)
    gs = jnp.concatenate(groups, axis=0)
    gsel = jnp.zeros((N_GROUPS, tm), F32)
    for _ in range(TOPK_GROUPS):
        m = jnp.max(gs, axis=0, keepdims=True)
        gi = jnp.min(jnp.where(gs == m, sub, N_GROUPS), axis=0, keepdims=True)
        hit = sub == gi
        gsel = jnp.where(hit, 1.0, gsel)
        gs = jnp.where(hit, ninf, gs)
    masked = jnp.concatenate(
        [jnp.where(gsel[g:g + 1, :] > 0.0, sel[g * GROUP_SIZE:(g + 1) * GROUP_SIZE, :], ninf)
         for g in range(N_GROUPS)], axis=0)
    row = lax.broadcasted_iota(I32, (N_EXPERTS, tm), 0)
    ids, ws = [], []
    for _ in range(TOP_K):
        m = jnp.max(masked, axis=0, keepdims=True)
        ii = jnp.min(jnp.where(masked == m, row, N_EXPERTS), axis=0, keepdims=True)
        hit = row == ii
        ws.append(jnp.sum(jnp.where(hit, scores, 0.0), axis=0, keepdims=True))
        ids.append(ii)
        masked = jnp.where(hit, ninf, masked)
    w = jnp.concatenate(ws, axis=0)
    w = w / jnp.sum(w, axis=0, keepdims=True) * ROUTED_SCALE
    return jnp.concatenate(ids, axis=0), w


def _post_mixer_kernel(*refs, n_parts, split_tile):
    i = pl.program_id(0)
    ys = refs[:2 * n_parts]
    ws = refs[2 * n_parts:3 * n_parts]
    (x_ref, gt_ref, g_ref, sc_ref, sh_ref, gtf_ref, rw_ref, rb_ref, s13_ref, s2_ref,
     xs_ref, h_ref, idx_ref, wts_ref, cnt_ref) = refs[3 * n_parts:]
    acc = None
    for p in range(n_parts):
        y = jnp.where(i < split_tile, ys[2 * p][...], ys[2 * p + 1][...])
        d = _dot(y.astype(BF16), ws[p][...])
        acc = d if acc is None else acc + d
    x = x_ref[...] + gt_ref[...] * acc
    h = _norm_mod(x, g_ref[...], sc_ref[...], sh_ref[...])
    h_ref[...] = h
    hh, hl = _split(h)
    rh, rl = _split(rw_ref[...])
    logits = _dot_nt(rh, hh) + _dot_nt(rl, hh) + _dot_nt(rh, hl)
    ids, w = _route(jax.nn.sigmoid(logits), rb_ref[...])
    idx_ref[...] = ids
    wts_ref[...] = w
    _, member = _membership(ids, ids.shape[1])
    cnt = jnp.sum(member, axis=1, keepdims=True)
    cnt_ref[...] = jnp.broadcast_to(cnt, cnt_ref.shape).astype(I32)
    u = _dot(hh, s13_ref[...])
    hs = _silu(u[:, :D_SHARED]) * u[:, D_SHARED:]
    xs_ref[...] = x + gtf_ref[...] * _dot(hs.astype(BF16), s2_ref[...])


def _post_mixer(parts, weights, x, gt, g, sc, sh, gtf, rw_t, rb, s13, s2, bounds, split_tile, name):
    t, d = x.shape
    nt = t // TM
    seq = lambda i: (_seq_of_tile(i, bounds), 0, 0)
    in_specs, args = [], []
    for yp, ys in parts:
        w = yp.shape[1]
        in_specs += [pl.BlockSpec((TM, w), lambda i: (jnp.minimum(i, split_tile - 1), 0)),
                     pl.BlockSpec((TM, w), lambda i: (jnp.maximum(i - split_tile, 0), 0))]
        args += [yp, ys]
    for w in weights:
        in_specs.append(pl.BlockSpec(w.shape, lambda i: (0, 0)))
        args.append(w)
    modspec = pl.BlockSpec((None, 1, d), seq)
    full = lambda a: pl.BlockSpec(a.shape, lambda i: (0, 0))
    in_specs += [pl.BlockSpec((TM, d), lambda i: (i, 0)), modspec, full(g), modspec, modspec, modspec,
                 full(rw_t), full(rb), full(s13), full(s2)]
    args += [x, gt, g, sc, sh, gtf, rw_t, rb, s13, s2]
    return pl.pallas_call(
        functools.partial(_post_mixer_kernel, n_parts=len(parts), split_tile=split_tile),
        grid=(nt,),
        in_specs=in_specs,
        out_specs=[pl.BlockSpec((TM, d), lambda i: (i, 0)),
                   pl.BlockSpec((TM, d), lambda i: (i, 0)),
                   pl.BlockSpec((TOP_K, TM), lambda i: (0, i)),
                   pl.BlockSpec((TOP_K, TM), lambda i: (0, i)),
                   pl.BlockSpec((None, N_EXPERTS, LANES), lambda i: (i, 0, 0))],
        out_shape=[jax.ShapeDtypeStruct((t, d), F32), jax.ShapeDtypeStruct((t, d), F32),
                   jax.ShapeDtypeStruct((TOP_K, t), I32), jax.ShapeDtypeStruct((TOP_K, t), F32),
                   jax.ShapeDtypeStruct((nt, N_EXPERTS, LANES), I32)],
        compiler_params=_cparams("parallel"), name=name,
    )(*args)


def _membership(ids, n_tok):
    row = lax.broadcasted_iota(I32, (N_EXPERTS, n_tok), 0)
    m = jnp.zeros((N_EXPERTS, n_tok), F32)
    for k in range(TOP_K):
        m = m + (row == ids[k:k + 1, :]).astype(F32)
    return row, m


def _dispatch_tables(cnt_tile):
    cnt = cnt_tile[:, :, 0]
    nt = cnt.shape[0]
    carry = jnp.cumsum(cnt, axis=0) - cnt
    counts = jnp.sum(cnt, axis=0)
    padded = (counts + EXPERT_CHUNK - 1) // EXPERT_CHUNK * EXPERT_CHUNK
    pends = jnp.cumsum(padded)
    pstarts = pends - padded
    base = (pstarts[None, :] + carry).astype(F32).reshape(nt, N_EXPERTS, 1)
    n_slots = nt * TM * TOP_K + N_EXPERTS * EXPERT_CHUNK
    n_chunks = n_slots // EXPERT_CHUNK
    chunk_e = jnp.minimum(
        jnp.searchsorted(pends, jnp.arange(n_chunks, dtype=I32) * EXPERT_CHUNK, side='right'),
        N_EXPERTS - 1).astype(I32)
    n_used = (pends[-1] // EXPERT_CHUNK).astype(I32).reshape(1)
    pad_lo = (pstarts + counts).astype(I32)
    return base, chunk_e, n_used, pad_lo, pends.astype(I32), n_chunks


def _rank_kernel(ids_ref, base_ref, tri_ref, slot_ref):
    ids = ids_ref[...]
    tm = ids.shape[1]
    row, m = _membership(ids, tm)
    prefix = _dot(m.astype(BF16), tri_ref[...])
    pos = base_ref[...] + (prefix - m)
    slots = [jnp.sum(jnp.where(row == ids[k:k + 1, :], pos, 0.0), axis=0, keepdims=True) for k in range(TOP_K)]
    slot_ref[...] = jnp.concatenate(slots, axis=0).astype(I32)


def _rank(ids, base):
    k, t = ids.shape
    tri = jnp.asarray(np.triu(np.ones((TM, TM), np.float32)).astype(ml_dtypes.bfloat16))
    return pl.pallas_call(
        _rank_kernel, grid=(t // TM,),
        in_specs=[pl.BlockSpec((k, TM), lambda i: (0, i)),
                  pl.BlockSpec((None, N_EXPERTS, 1), lambda i: (i, 0, 0)),
                  pl.BlockSpec((TM, TM), lambda i: (0, 0))],
        out_specs=pl.BlockSpec((k, TM), lambda i: (0, i)),
        out_shape=jax.ShapeDtypeStruct((k, t), I32),
        compiler_params=_cparams("parallel"), name="moe_rank",
    )(ids, base, tri)


PUSH_UNROLL = 4


def _push_kernel(lo_ref, hi_ref, nu_ref, slot_ref, h_ref, xs_hbm, zero, sem):
    i = pl.program_id(0)
    tm = h_ref.shape[0]
    n_chunks = xs_hbm.shape[0] // EXPERT_CHUNK

    def body(tb, carry):
        for j in range(PUSH_UNROLL):
            t = tb * PUSH_UNROLL + j
            for k in range(TOP_K):
                pltpu.make_async_copy(h_ref.at[pl.ds(t, 1), :], xs_hbm.at[pl.ds(slot_ref[k, t], 1), :], sem).start()
        return carry

    lax.fori_loop(0, tm // PUSH_UNROLL, body, 0)

    @pl.when(i == 0)
    def _():
        zero[...] = jnp.zeros_like(zero)

        def zstart(r, carry):
            pltpu.make_async_copy(zero.at[pl.ds(0, 1), :], xs_hbm.at[pl.ds(r, 1), :], sem).start()
            return carry

        def zwait(r, carry):
            pltpu.make_async_copy(zero.at[pl.ds(0, 1), :], xs_hbm.at[pl.ds(0, 1), :], sem).wait()
            return carry

        def per_expert(e, carry):
            lax.fori_loop(lo_ref[e], hi_ref[e], zstart, 0)
            lax.fori_loop(lo_ref[e], hi_ref[e], zwait, 0)
            return carry

        lax.fori_loop(0, N_EXPERTS, per_expert, 0)

        def chunk_copy(c):
            row0 = pl.multiple_of(c * EXPERT_CHUNK, EXPERT_CHUNK)
            return pltpu.make_async_copy(zero, xs_hbm.at[pl.ds(row0, EXPERT_CHUNK), :], sem)

        def cstart(c, carry):
            chunk_copy(c).start()
            return carry

        def cwait(c, carry):
            chunk_copy(c).wait()
            return carry

        lax.fori_loop(nu_ref[0], n_chunks, cstart, 0)
        lax.fori_loop(nu_ref[0], n_chunks, cwait, 0)

    for _ in range(TOP_K):
        pltpu.make_async_copy(h_ref, xs_hbm.at[pl.ds(0, tm), :], sem).wait()


def _push(slot_of, h, pad_lo, pad_hi, n_used, n_slots):
    t, d = h.shape
    nt = t // TM
    grid_spec = pltpu.PrefetchScalarGridSpec(
        num_scalar_prefetch=3,
        grid=(nt,),
        in_specs=[pl.BlockSpec((TOP_K, TM), lambda i, lo, hi, nu: (0, i), memory_space=pltpu.SMEM),
                  pl.BlockSpec((TM, d), lambda i, lo, hi, nu: (i, 0))],
        out_specs=pl.BlockSpec(memory_space=pl.ANY),
        scratch_shapes=[pltpu.VMEM((EXPERT_CHUNK, d), F32), pltpu.SemaphoreType.DMA(())],
    )
    return pl.pallas_call(
        _push_kernel, grid_spec=grid_spec,
        out_shape=jax.ShapeDtypeStruct((n_slots, d), F32),
        compiler_params=_cparams("arbitrary"), name="moe_push",
    )(pad_lo, pad_hi, n_used, slot_of, h)


def _expert_kernel(ce_ref, nu_ref, x_ref, w1_ref, w3_ref, w2_ref, o_ref):
    c = pl.program_id(0)

    @pl.when(c < nu_ref[0])
    def _():
        x = x_ref[...].astype(BF16)
        a = _dot(x, w1_ref[...])
        b = _dot(x, w3_ref[...])
        o_ref[...] = _dot((_silu(a) * b).astype(BF16), w2_ref[...])

    @pl.when(c >= nu_ref[0])
    def _():
        o_ref[...] = jnp.zeros_like(o_ref)


def _expert_ffn(xs, chunk_e, n_used, w1, w3, w2):
    n_slots, d = xs.shape
    n_chunks = n_slots // EXPERT_CHUNK
    live = lambda c, nu: jnp.minimum(c, nu[0] - 1)
    grid_spec = pltpu.PrefetchScalarGridSpec(
        num_scalar_prefetch=2,
        grid=(n_chunks,),
        in_specs=[
            pl.BlockSpec((EXPERT_CHUNK, d), lambda c, ce, nu: (live(c, nu), 0)),
            pl.BlockSpec((None, d, D_EXPERT), lambda c, ce, nu: (ce[live(c, nu)], 0, 0)),
            pl.BlockSpec((None, d, D_EXPERT), lambda c, ce, nu: (ce[live(c, nu)], 0, 0)),
            pl.BlockSpec((None, D_EXPERT, d), lambda c, ce, nu: (ce[live(c, nu)], 0, 0)),
        ],
        out_specs=pl.BlockSpec((EXPERT_CHUNK, d), lambda c, ce, nu: (c, 0)),
    )
    return pl.pallas_call(
        _expert_kernel, grid_spec=grid_spec,
        out_shape=jax.ShapeDtypeStruct((n_slots, d), F32),
        compiler_params=_cparams("arbitrary"), name="moe_experts",
    )(chunk_e, n_used, xs, w1, w3, w2)


def _combine_kernel(sc_ref, sn_ref, w_ref, xs_ref, gt_ref, ys_hbm, o_ref, buf, sem):
    i = pl.program_id(0)
    n = pl.num_programs(0)
    slot = i % 2

    def issue(s_ref, s):
        def body(rb, carry):
            for j in range(SUBLANES):
                r = rb * SUBLANES + j
                for k in range(TOP_K):
                    pltpu.make_async_copy(ys_hbm.at[pl.ds(s_ref[k, r], 1), :],
                                          buf.at[s, k, pl.ds(r, 1), :], sem.at[s]).start()
            return carry
        lax.fori_loop(0, TC // SUBLANES, body, 0)

    @pl.when(i == 0)
    def _():
        issue(sc_ref, 0)

    @pl.when(i + 1 < n)
    def _():
        issue(sn_ref, 1 - slot)

    for k in range(TOP_K):
        pltpu.make_async_copy(ys_hbm.at[pl.ds(0, TC), :], buf.at[slot, k], sem.at[slot]).wait()
    d = o_ref.shape[1]
    acc = None
    for k in range(TOP_K):
        wcol = jnp.transpose(jnp.broadcast_to(w_ref[k:k + 1, :], (LANES, TC)))
        term = buf[slot, k] * _lane_tile(wcol, d)
        acc = term if acc is None else acc + term
    o_ref[...] = xs_ref[...] + gt_ref[...] * acc


def _combine(slot_of, wts, xs, gtf, ys, bounds_tc):
    t, d = xs.shape
    nt = t // TC
    seq = lambda i: (_seq_of_tile(i, bounds_tc), 0, 0)
    return pl.pallas_call(
        _combine_kernel, grid=(nt,),
        in_specs=[pl.BlockSpec((TOP_K, TC), lambda i: (0, i), memory_space=pltpu.SMEM),
                  pl.BlockSpec((TOP_K, TC), lambda i: (0, jnp.minimum(i + 1, nt - 1)), memory_space=pltpu.SMEM),
                  pl.BlockSpec((TOP_K, TC), lambda i: (0, i)),
                  pl.BlockSpec((TC, d), lambda i: (i, 0)),
                  pl.BlockSpec((None, 1, d), seq),
                  pl.BlockSpec(memory_space=pl.ANY)],
        out_specs=pl.BlockSpec((TC, d), lambda i: (i, 0)),
        out_shape=jax.ShapeDtypeStruct((t, d), F32),
        scratch_shapes=[pltpu.VMEM((2, TOP_K, TC, d), F32), pltpu.SemaphoreType.DMA((2,))],
        compiler_params=_cparams("arbitrary"), name="moe_combine",
    )(slot_of, slot_of, wts, xs, gtf, ys)


def _final_norm_kernel(x_ref, g_ref, o_ref):
    x = x_ref[...]
    ms = jnp.mean(x * x, axis=-1, keepdims=True)
    o_ref[...] = x * lax.rsqrt(ms + EPS) * g_ref[...]


def _final_norm(x, g, tile_off, n_tiles):
    d = x.shape[1]
    return pl.pallas_call(
        _final_norm_kernel, grid=(n_tiles,),
        in_specs=[pl.BlockSpec((TM, d), lambda i: (i + tile_off, 0)), pl.BlockSpec((1, d), lambda i: (0, 0))],
        out_specs=pl.BlockSpec((TM, d), lambda i: (i, 0)),
        out_shape=jax.ShapeDtypeStruct((n_tiles * TM, d), F32),
        compiler_params=_cparams("parallel"), name="final_norm",
    )(x, g)


def kernel(x_prompt, x_sample, c_prompt, c_sample, ada_w, ada_b, g_mix, g_ffn, g_final,
           w_in_ab, w_out_ab, conv_w, conv_b, filt_w1, filt_b1, filt_f1, filt_w2, filt_b2, filt_f2,
           filt_w3, decay, hyena_bias, w_qkv, w_out_c, rpb,
           router_w, router_b, exp_w1, exp_w3, exp_w2, sh_w1, sh_w3, sh_w2):
    d = D_MODEL
    bp, lp, _ = x_prompt.shape
    bs, ls, _ = x_sample.shape
    trunks = ((0, bp, lp), (bp * lp, bs, ls))
    t_all = bp * lp + bs * ls
    n_seq = bp + bs
    depth = ada_w.shape[0]
    assert lp % TM == 0 and ls % TM == 0
    bounds, acc = [], 0
    for _, b, l in trunks:
        for _ in range(b):
            acc += l // TM
            bounds.append(acc)
    bounds = tuple(bounds)
    bounds_tc = tuple(b * (TM // TC) for b in bounds)
    split_tile = bp * lp // TM

    x = jnp.concatenate([x_prompt.reshape(bp * lp, d), x_sample.reshape(bs * ls, d)], axis=0)
    c_all = jnp.concatenate([c_prompt, c_sample, jnp.zeros((SUBLANES - n_seq, d), F32)], axis=0)
    mod = _modulation(c_all, ada_w, ada_b)[:, :n_seq]
    mod = mod.reshape(depth, n_seq, 6, 1, d)

    bands = jnp.linspace(1e-4, POS_BANDS - 1, POS_BANDS, dtype=F32).reshape(1, POS_BANDS)
    bias_tab = None

    for l in range(depth):
        sh_m, sc_m, gt_m, sh_f, sc_f, gt_f = (mod[l, :, j] for j in range(6))
        g_m = g_mix[l].reshape(1, d)
        i = l // 2
        if l % 2 == 0:
            u_f, u_h = _norm_mod_matmul(x, g_m, sc_m, sh_m, w_in_ab[i].astype(BF16),
                                        (D_FNET, 3 * D_HYENA), (F32, F32), bounds, "in_proj_ab")
            z, x0 = _hyena_pre(u_h, conv_w[i], conv_b[i], bounds)
            ya, yb = [], []
            for tok_off, b, sl in trunks:
                plan = _HyenaPlan(sl)
                hf, hb, ss = _hyena_filter(sl, bands, filt_w1[i], filt_b1[i], filt_f1[i], filt_w2[i],
                                           filt_b2[i], filt_f2[i], filt_w3[i], decay[i])
                hc = jnp.concatenate([hf, jnp.zeros((1, D_HYENA), F32), hb[1:][::-1]], axis=0)
                hr, hi = _hyena_spectrum(plan, hc)
                yb.append(_hyena_longconv(plan, z, x0, hyena_bias[i], hr, hi, ss, tok_off, b, sl))
                ya.append(_fnet_mix(u_f, tok_off, b, sl))
            parts = [tuple(ya), tuple(yb)]
            w_out = w_out_ab[i].astype(BF16)
            weights = [w_out[:D_FNET], w_out[D_FNET:]]
        else:
            q, k, v = _norm_mod_matmul(x, g_m, sc_m, sh_m, w_qkv[i].astype(BF16),
                                       (d, d, d), (BF16, BF16, BF16), bounds, "qkv_proj")
            bias_tab = _attn_bias_table(rpb[i])
            o = tuple(_attention(q, k, v, bias_tab, tok_off, b, sl) for tok_off, b, sl in trunks)
            parts = [o]
            weights = [w_out_c[i].astype(BF16)]
        s13 = jnp.concatenate([sh_w1[l], sh_w3[l]], axis=1).astype(BF16)
        xs, h, idx, wts, cnt_tile = _post_mixer(
            parts, weights, x, gt_m, g_ffn[l].reshape(1, d), sc_f, sh_f, gt_f,
            router_w[l].T, router_b[l].reshape(N_EXPERTS, 1), s13, sh_w2[l].astype(BF16),
            bounds, split_tile, "post_mixer_%d" % l)
        base, chunk_e, n_used, pad_lo, pad_hi, n_chunks = _dispatch_tables(cnt_tile)
        slot_of = _rank(idx, base)
        xg = _push(slot_of, h, pad_lo, pad_hi, n_used, n_chunks * EXPERT_CHUNK)
        ys = _expert_ffn(xg, chunk_e, n_used,
                         exp_w1[l].astype(BF16), exp_w3[l].astype(BF16), exp_w2[l].astype(BF16))
        x = _combine(slot_of, wts, xs, gt_f, ys, bounds_tc)

    g_fin = g_final.reshape(1, d)
    y_prompt = _final_norm(x, g_fin, 0, bp * lp // TM).reshape(bp, lp, d)
    y_sample = _final_norm(x, g_fin, bp * lp // TM, bs * ls // TM).reshape(bs, ls, d)
    return (y_prompt, y_sample)
```

```python
import functools
import math

import ml_dtypes
import numpy as np

import jax
import jax.numpy as jnp
from jax import lax
from jax.experimental import pallas as pl
from jax.experimental.pallas import tpu as pltpu

F32 = jnp.float32
BF16 = jnp.bfloat16
I32 = jnp.int32

D_MODEL = 1024
GRID_W = 64
D_FNET = 512
N_FNET_GROUPS = 4
D_GROUP = D_FNET // N_FNET_GROUPS
D_HYENA = 512
POS_BANDS = 16
N_HEADS = 16
HEAD_DIM = 64
WIN_ROWS = 8
WIN_COLS = 16
N_EXPERTS = 64
TOP_K = 8
N_GROUPS = 8
GROUP_SIZE = N_EXPERTS // N_GROUPS
TOPK_GROUPS = 4
D_EXPERT = 256
D_SHARED = 256
ROUTED_SCALE = 2.5
EXPERT_CHUNK = 256
EPS = 1e-6
NEG_BIG = -1e30

LANES = 128
SUBLANES = 8
TM = 512
TC = 128
NB = 8
VMEM_LIMIT = 56 * 1024 * 1024


def _cparams(*sem):
    return pltpu.CompilerParams(dimension_semantics=sem, vmem_limit_bytes=VMEM_LIMIT)


def _dot(a, b):
    return jnp.dot(a, b, preferred_element_type=F32)


def _dot_nt(a, b):
    return lax.dot_general(a, b, (((1,), (1,)), ((), ())), preferred_element_type=F32)


def _split(x):
    hi = x.astype(BF16)
    lo = (x - hi.astype(F32)).astype(BF16)
    return hi, lo


def _dot3_lc(mh, ml, x):
    xh, xl = _split(x)
    return _dot(mh, xh) + _dot(mh, xl) + _dot(ml, xh)


def _dot3_rc(x, mh, ml):
    xh, xl = _split(x)
    return _dot(xh, mh) + _dot(xl, mh) + _dot(xh, ml)


def _dot3(a, b):
    ah, al = _split(a)
    bh, bl = _split(b)
    return _dot(ah, bh) + _dot(al, bh) + _dot(ah, bl)


def _silu(x):
    return x * jax.nn.sigmoid(x)


def _lane_tile(t, width):
    return jnp.concatenate([t] * (width // LANES), axis=-1)


def _cmul(ar, ai, br, bi):
    return ar * br - ai * bi, ar * bi + ai * br


def _norm_mod(x, g, sc, sh):
    ms = jnp.mean(x * x, axis=-1, keepdims=True)
    return (x * lax.rsqrt(ms + EPS) * g) * (1.0 + sc) + sh


def _seq_of_tile(i, bounds):
    s = jnp.int32(0)
    for b in bounds[:-1]:
        s = s + (i >= b).astype(I32)
    return s


def _hilo(m):
    m = np.asarray(m, np.float64)
    hi = m.astype(np.float32).astype(ml_dtypes.bfloat16)
    lo = (m - hi.astype(np.float64)).astype(np.float32).astype(ml_dtypes.bfloat16)
    return jnp.asarray(hi), jnp.asarray(lo)


def _cos_sin(n):
    j = np.arange(n)
    ang = 2.0 * np.pi * ((j[:, None] * j[None, :]) % n) / n
    return np.cos(ang), np.sin(ang)


def _twiddle(n_slab, n_row, n_total):
    s = jnp.arange(n_slab, dtype=I32)[:, None]
    r = jnp.arange(n_row, dtype=I32)[None, :]
    ang = ((s * r) % n_total).astype(F32) * (2.0 * math.pi / n_total)
    shape = (n_slab, n_row, LANES)
    return (jnp.broadcast_to(jnp.cos(ang)[:, :, None], shape),
            jnp.broadcast_to(-jnp.sin(ang)[:, :, None], shape))


def _mod_kernel(c_ref, w_ref, b_ref, o_ref):
    o_ref[...] = _dot3(_silu(c_ref[...]), w_ref[...]) + b_ref[...]


def _modulation(c_pad, ada_w, ada_b):
    depth, d, n = ada_w.shape
    tn = 1536
    return pl.pallas_call(
        _mod_kernel,
        grid=(depth, n // tn),
        in_specs=[pl.BlockSpec((SUBLANES, d), lambda l, j: (0, 0)),
                  pl.BlockSpec((None, d, tn), lambda l, j: (l, 0, j)),
                  pl.BlockSpec((None, 1, tn), lambda l, j: (l, 0, j))],
        out_specs=pl.BlockSpec((None, SUBLANES, tn), lambda l, j: (l, 0, j)),
        out_shape=jax.ShapeDtypeStruct((depth, SUBLANES, n), F32),
        compiler_params=_cparams("parallel", "parallel"),
        name="adaln_mod",
    )(c_pad, ada_w, ada_b.reshape(depth, 1, n))


def _pair_specs(pair, split_tile):
    a, b = pair
    w = a.shape[1]
    b_off = 0 if a is b else split_tile
    return [pl.BlockSpec((TM, w), lambda i: (jnp.minimum(i, split_tile - 1), 0)),
            pl.BlockSpec((TM, w), lambda i: (jnp.maximum(i, split_tile) - b_off, 0))]


def _pair_read(a_ref, b_ref, split_tile):
    return jnp.where(pl.program_id(0) < split_tile, a_ref[...], b_ref[...])


def _nmm_kernel(xa_ref, xb_ref, g_ref, sc_ref, sh_ref, w_ref, *o_refs, split_tile):
    x = _pair_read(xa_ref, xb_ref, split_tile)
    h = _norm_mod(x, g_ref[...], sc_ref[...], sh_ref[...])
    u = _dot(h.astype(BF16), w_ref[...])
    off = 0
    for o in o_refs:
        n = o.shape[-1]
        o[...] = u[:, off:off + n].astype(o.dtype)
        off += n


def _norm_mod_matmul(x_pair, g, sc, sh, w_bf16, splits, out_dtypes, bounds, split_tile, name):
    d, n = w_bf16.shape
    nt = bounds[-1]
    seq = lambda i: (_seq_of_tile(i, bounds), 0, 0)
    return pl.pallas_call(
        functools.partial(_nmm_kernel, split_tile=split_tile),
        grid=(nt,),
        in_specs=_pair_specs(x_pair, split_tile) + [
            pl.BlockSpec((1, d), lambda i: (0, 0)),
            pl.BlockSpec((None, 1, d), seq),
            pl.BlockSpec((None, 1, d), seq),
            pl.BlockSpec((d, n), lambda i: (0, 0))],
        out_specs=[pl.BlockSpec((TM, s), lambda i: (i, 0)) for s in splits],
        out_shape=[jax.ShapeDtypeStruct((nt * TM, s), dt) for s, dt in zip(splits, out_dtypes)],
        compiler_params=_cparams("parallel"),
        name=name,
    )(*x_pair, g, sc, sh, w_bf16)


def _hyena_pre_kernel(prev_ref, cur_ref, next_ref, cw_ref, cb_ref, z_ref, x0_ref, *, first_tiles, last_tiles):
    i = pl.program_id(0)
    cur = cur_ref[...]
    tm = cur.shape[0]
    is_first = functools.reduce(jnp.logical_or, [i == f for f in first_tiles])
    is_last = functools.reduce(jnp.logical_or, [i == f for f in last_tiles])
    prow = jnp.where(is_first, 0.0, prev_ref[SUBLANES - 1:SUBLANES, :])
    nrow = jnp.where(is_last, 0.0, next_ref[0:1, :])
    rid = lax.broadcasted_iota(I32, (tm, 1), 0)
    up = jnp.where(rid == 0, prow, pltpu.roll(cur, 1, 0))
    dn = jnp.where(rid == tm - 1, nrow, pltpu.roll(cur, tm - 1, 0))
    cw = cw_ref[...]
    uc = cb_ref[...] + (up * cw[0:1, :] + cur * cw[1:2, :] + dn * cw[2:3, :])
    x0_ref[...] = uc[:, :D_HYENA]
    z_ref[...] = uc[:, 2 * D_HYENA:] * uc[:, D_HYENA:2 * D_HYENA]


def _hyena_pre(u_h, conv_w, conv_b, bounds):
    t, c = u_h.shape
    nt = t // TM
    rb = TM // SUBLANES
    first_tiles = (0,) + tuple(bounds[:-1])
    last_tiles = tuple(b - 1 for b in bounds)
    kern = functools.partial(_hyena_pre_kernel, first_tiles=first_tiles, last_tiles=last_tiles)
    return pl.pallas_call(
        kern,
        grid=(nt,),
        in_specs=[pl.BlockSpec((SUBLANES, c), lambda i: (jnp.maximum(i * rb - 1, 0), 0)),
                  pl.BlockSpec((TM, c), lambda i: (i, 0)),
                  pl.BlockSpec((SUBLANES, c), lambda i: (jnp.minimum((i + 1) * rb, nt * rb - 1), 0)),
                  pl.BlockSpec((3, c), lambda i: (0, 0)),
                  pl.BlockSpec((1, c), lambda i: (0, 0))],
        out_specs=[pl.BlockSpec((TM, D_HYENA), lambda i: (i, 0)),
                   pl.BlockSpec((TM, D_HYENA), lambda i: (i, 0))],
        out_shape=[jax.ShapeDtypeStruct((t, D_HYENA), F32), jax.ShapeDtypeStruct((t, D_HYENA), F32)],
        compiler_params=_cparams("parallel"),
        name="hyena_pre",
    )(u_h, u_h, u_h, conv_w, conv_b.reshape(1, c))


def _filter_kernel(bands_ref, w1_ref, b1_ref, f1_ref, w2_ref, b2_ref, f2_ref, w3_ref, dec_ref,
                   hc_ref, ss_ref, *, seq_len):
    i = pl.program_id(0)
    tm = hc_ref.shape[0]
    row = i * tm + lax.broadcasted_iota(I32, (tm, 1), 0)
    is_bwd = i >= seq_len // tm
    n = jnp.where(is_bwd, 2 * seq_len - row, row).astype(F32)
    t = n / float(seq_len - 1)
    ang = (2.0 * math.pi / seq_len) * n * bands_ref[...]
    w1 = w1_ref[...]
    pre = (t * w1[0:1, :] + _dot3(jnp.cos(ang), w1[1:1 + POS_BANDS, :])
           + _dot3(-jnp.sin(ang), w1[1 + POS_BANDS:, :]) + b1_ref[...])
    h = jnp.sin(f1_ref[...] * pre)
    h = jnp.sin(f2_ref[...] * (_dot3(h, w2_ref[...]) + b2_ref[...]))
    w3 = w3_ref[...]
    dec = jnp.abs(dec_ref[...])
    h = _dot3(h, jnp.where(is_bwd, w3[:, D_HYENA:], w3[:, :D_HYENA]))
    hc = h * jnp.exp(-t * jnp.where(is_bwd, dec[1:2, :], dec[0:1, :]))
    hc = jnp.where(row == seq_len, 0.0, hc)
    hc_ref[...] = hc

    @pl.when(i == 0)
    def _():
        ss_ref[...] = jnp.zeros_like(ss_ref)

    ss_ref[...] += jnp.sum(hc * hc, axis=0, keepdims=True)


def _hyena_filter(seq_len, bands, w1, b1, f1, w2, b2, f2, w3, decay):
    tm = 512
    full = lambda a: pl.BlockSpec(a.shape, lambda i: (0,) * a.ndim)
    args = (bands, w1, b1.reshape(1, -1), f1.reshape(1, -1), w2, b2.reshape(1, -1), f2.reshape(1, -1), w3, decay)
    return pl.pallas_call(
        functools.partial(_filter_kernel, seq_len=seq_len),
        grid=(2 * seq_len // tm,),
        in_specs=[full(a) for a in args],
        out_specs=[pl.BlockSpec((tm, D_HYENA), lambda i: (i, 0)),
                   pl.BlockSpec((1, D_HYENA), lambda i: (0, 0))],
        out_shape=[jax.ShapeDtypeStruct((2 * seq_len, D_HYENA), F32),
                   jax.ShapeDtypeStruct((1, D_HYENA), F32)],
        compiler_params=_cparams("arbitrary"),
        name="hyena_filter",
    )(*args)


def _strided_fwd_kernel(x_ref, mh_ref, ml_ref, or_ref, oi_ref):
    n1 = or_ref.shape[0]
    mh = mh_ref[...]
    ml = ml_ref[...]
    for j in range(NB):
        a = _dot3_lc(mh, ml, x_ref[:, j, :])
        or_ref[:, j, :] = a[:n1]
        oi_ref[:, j, :] = a[n1:]


def _strided_fwd(x3, row_block, rows_in, m, batch, name):
    _, n2, ch = x3.shape
    mh, ml = m
    n1 = mh.shape[0] // 2
    out = jax.ShapeDtypeStruct((batch, n1, n2, ch), F32)
    ospec = pl.BlockSpec((None, n1, NB, ch), lambda b, j: (b, 0, j, 0))
    return pl.pallas_call(
        _strided_fwd_kernel,
        grid=(batch, n2 // NB),
        in_specs=[pl.BlockSpec((rows_in, NB, ch), lambda b, j: (row_block + b, j, 0)),
                  pl.BlockSpec(mh.shape, lambda b, j: (0, 0)),
                  pl.BlockSpec(ml.shape, lambda b, j: (0, 0))],
        out_specs=[ospec, ospec],
        out_shape=[out, out],
        compiler_params=_cparams("parallel", "parallel"),
        name=name,
    )(x3, mh, ml)


def _slab_specs(n1, n2, ch, batched):
    if batched:
        return pl.BlockSpec((None, NB, n2, ch), lambda b, k: (b, k, 0, 0))
    return pl.BlockSpec((None, NB, n2, ch), lambda b, k: (0, k, 0, 0))


def _slab_fwd_kernel(ar_ref, ai_ref, twr_ref, twi_ref, fh_ref, fl_ref, or_ref, oi_ref):
    n2, ch = ar_ref.shape[1:]
    fh = fh_ref[...]
    fl = fl_ref[...]
    for s in range(NB):
        twr = _lane_tile(twr_ref[s], ch)
        twi = _lane_tile(twi_ref[s], ch)
        br, bi = _cmul(ar_ref[s], ai_ref[s], twr, twi)
        x = _dot3_lc(fh, fl, jnp.concatenate([br, bi], axis=0))
        or_ref[s] = x[:n2]
        oi_ref[s] = x[n2:]


def _slab_conv_kernel(ar_ref, ai_ref, hr_ref, hi_ref, ss_ref, twr_ref, twi_ref,
                      fh_ref, fl_ref, gh_ref, gl_ref, or_ref, oi_ref):
    n2, ch = ar_ref.shape[1:]
    fh = fh_ref[...]
    fl = fl_ref[...]
    gh = gh_ref[...]
    gl = gl_ref[...]
    scale = lax.rsqrt(ss_ref[...] + EPS)
    for s in range(NB):
        twr = _lane_tile(twr_ref[s], ch)
        twi = _lane_tile(twi_ref[s], ch)
        br, bi = _cmul(ar_ref[s], ai_ref[s], twr, twi)
        x = _dot3_lc(fh, fl, jnp.concatenate([br, bi], axis=0))
        pr, pi = _cmul(x[:n2], x[n2:], hr_ref[s] * scale, hi_ref[s] * scale)
        g = _dot3_lc(gh, gl, jnp.concatenate([pr, pi], axis=0))
        qr, qi = _cmul(g[:n2], g[n2:], twr, -twi)
        or_ref[s] = qr
        oi_ref[s] = qi


def _hyena_out_kernel(gr_ref, gi_ref, z_ref, x0_ref, hb_ref, mh_ref, ml_ref, o_ref):
    mh = mh_ref[...]
    ml = ml_ref[...]
    hb = hb_ref[...]
    for j in range(NB):
        g = jnp.concatenate([gr_ref[:, j, :], gi_ref[:, j, :]], axis=0)
        zc = _dot3_lc(mh, ml, g)
        o_ref[:, j, :] = x0_ref[:, j, :] * (zc + z_ref[:, j, :] * hb)


class _HyenaPlan:
    def __init__(self, seq_len):
        n = 2 * seq_len
        self.n = n
        self.n1 = 256 if n >= 32768 else 128
        self.n2 = n // self.n1
        self.half = self.n1 // 2
        c1, s1 = _cos_sin(self.n1)
        c2, s2 = _cos_sin(self.n2)
        h = self.half
        self.m1 = _hilo(np.concatenate([c1, -s1], axis=0))
        self.m1_half = _hilo(np.concatenate([c1[:, :h], -s1[:, :h]], axis=0))
        self.f2 = _hilo(np.block([[c2, s2], [-s2, c2]]))
        self.g2 = _hilo(np.block([[c2, -s2], [s2, c2]]) / n)
        self.m2 = _hilo(np.concatenate([c1[:h], -s1[:h]], axis=1))


def _hyena_spectrum(plan, hc):
    n1, n2 = plan.n1, plan.n2
    ch = hc.shape[1]
    ar, ai = _strided_fwd(hc.reshape(n1, n2, ch), 0, n1, plan.m1, 1, "hyena_filt_s1")
    twr, twi = _twiddle(n1, n2, plan.n)
    fh, fl = plan.f2
    spec = _slab_specs(n1, n2, ch, True)
    tspec = pl.BlockSpec((NB, n2, LANES), lambda b, k: (k, 0, 0))
    cspec = pl.BlockSpec(fh.shape, lambda b, k: (0, 0))
    out = jax.ShapeDtypeStruct((1, n1, n2, ch), F32)
    return pl.pallas_call(
        _slab_fwd_kernel, grid=(1, n1 // NB),
        in_specs=[spec, spec, tspec, tspec, cspec, cspec],
        out_specs=[spec, spec], out_shape=[out, out],
        compiler_params=_cparams("parallel", "parallel"), name="hyena_filt_slab",
    )(ar, ai, twr, twi, fh, fl)


def _hyena_longconv(plan, z, x0, hyena_bias, hr, hi, ss, tok_off, batch, seq_len):
    n1, n2, half = plan.n1, plan.n2, plan.half
    t, ch = z.shape
    assert tok_off % seq_len == 0 and seq_len == half * n2
    row_block = tok_off // seq_len
    z3 = z.reshape(t // n2, n2, ch)
    x03 = x0.reshape(t // n2, n2, ch)
    ar, ai = _strided_fwd(z3, row_block, half, plan.m1_half, batch, "hyena_s1")
    twr, twi = _twiddle(n1, n2, plan.n)
    fh, fl = plan.f2
    gh, gl = plan.g2
    spec = _slab_specs(n1, n2, ch, True)
    hspec = _slab_specs(n1, n2, ch, False)
    tspec = pl.BlockSpec((NB, n2, LANES), lambda b, k: (k, 0, 0))
    cspec = pl.BlockSpec(fh.shape, lambda b, k: (0, 0))
    out = jax.ShapeDtypeStruct((batch, n1, n2, ch), F32)
    gr, gi = pl.pallas_call(
        _slab_conv_kernel, grid=(batch, n1 // NB),
        in_specs=[spec, spec, hspec, hspec, pl.BlockSpec((1, ch), lambda b, k: (0, 0)),
                  tspec, tspec, cspec, cspec, cspec, cspec],
        out_specs=[spec, spec], out_shape=[out, out],
        compiler_params=_cparams("parallel", "parallel"), name="hyena_slab_conv",
    )(ar, ai, hr, hi, ss, twr, twi, fh, fl, gh, gl)
    mh, ml = plan.m2
    gspec = pl.BlockSpec((None, n1, NB, ch), lambda b, j: (b, 0, j, 0))
    xspec = pl.BlockSpec((half, NB, ch), lambda b, j: (row_block + b, j, 0))
    y = pl.pallas_call(
        _hyena_out_kernel, grid=(batch, n2 // NB),
        in_specs=[gspec, gspec, xspec, xspec, pl.BlockSpec((1, ch), lambda b, j: (0, 0)),
                  pl.BlockSpec(mh.shape, lambda b, j: (0, 0)), pl.BlockSpec(ml.shape, lambda b, j: (0, 0))],
        out_specs=pl.BlockSpec((half, NB, ch), lambda b, j: (b, j, 0)),
        out_shape=jax.ShapeDtypeStruct((batch * half, n2, ch), F32),
        compiler_params=_cparams("parallel", "parallel"), name="hyena_s2",
    )(gr, gi, z3, x03, hyena_bias.reshape(1, ch), mh, ml)
    return y.reshape(batch * seq_len, ch)


def _fnet_s1_kernel(x_ref, wh_ref, wl_ref, mh_ref, ml_ref, or_ref, oi_ref):
    n1 = or_ref.shape[0]
    wh = wh_ref[...]
    wl = wl_ref[...]
    mh = mh_ref[...]
    ml = ml_ref[...]
    for j in range(NB):
        x = x_ref[:, j, :]
        zr, zi = [], []
        for g in range(N_FNET_GROUPS):
            zg = _dot3_rc(x[:, g * D_GROUP:(g + 1) * D_GROUP], wh, wl)
            zr.append(zg[:, :D_GROUP])
            zi.append(zg[:, D_GROUP:])
        z = jnp.concatenate([jnp.concatenate(zr, axis=1), jnp.concatenate(zi, axis=1)], axis=0)
        a = _dot3_lc(mh, ml, z)
        or_ref[:, j, :] = a[:n1]
        oi_ref[:, j, :] = a[n1:]


def _fnet_slab_kernel(ar_ref, ai_ref, twr_ref, twi_ref, fh_ref, fl_ref, o_ref, *, scale):
    ch = ar_ref.shape[2]
    fh = fh_ref[...]
    fl = fl_ref[...]
    for s in range(NB):
        twr = _lane_tile(twr_ref[s], ch)
        twi = _lane_tile(twi_ref[s], ch)
        br, bi = _cmul(ar_ref[s], ai_ref[s], twr, twi)
        o_ref[:, s, :] = _dot3_lc(fh, fl, jnp.concatenate([br, bi], axis=0)) * scale


def _fnet_mix(u_f, tok_off, batch, seq_len):
    t, ch = u_f.shape
    n2 = 128
    n1 = seq_len // n2
    assert tok_off % seq_len == 0
    row_block = tok_off // seq_len
    c1, s1 = _cos_sin(n1)
    c2, s2 = _cos_sin(n2)
    cg, sg = _cos_sin(D_GROUP)
    wh, wl = _hilo(np.concatenate([cg, -sg], axis=1))
    mh, ml = _hilo(np.block([[c1, s1], [-s1, c1]]))
    fh, fl = _hilo(np.concatenate([c2, s2], axis=1))
    u3 = u_f.reshape(t // n2, n2, ch)
    out = jax.ShapeDtypeStruct((batch, n1, n2, ch), F32)
    ospec = pl.BlockSpec((None, n1, NB, ch), lambda b, j: (b, 0, j, 0))
    c2d = lambda a: pl.BlockSpec(a.shape, lambda b, j: (0, 0))
    ar, ai = pl.pallas_call(
        _fnet_s1_kernel, grid=(batch, n2 // NB),
        in_specs=[pl.BlockSpec((n1, NB, ch), lambda b, j: (row_block + b, j, 0)),
                  c2d(wh), c2d(wl), c2d(mh), c2d(ml)],
        out_specs=[ospec, ospec], out_shape=[out, out],
        compiler_params=_cparams("parallel", "parallel"), name="fnet_s1",
    )(u3, wh, wl, mh, ml)
    twr, twi = _twiddle(n1, n2, seq_len)
    spec = _slab_specs(n1, n2, ch, True)
    tspec = pl.BlockSpec((NB, n2, LANES), lambda b, k: (k, 0, 0))
    scale = 1.0 / math.sqrt(seq_len * D_GROUP)
    y = pl.pallas_call(
        functools.partial(_fnet_slab_kernel, scale=scale), grid=(batch, n1 // NB),
        in_specs=[spec, spec, tspec, tspec, c2d(fh), c2d(fl)],
        out_specs=pl.BlockSpec((None, n2, NB, ch), lambda b, k: (b, 0, k, 0)),
        out_shape=jax.ShapeDtypeStruct((batch, n2, n1, ch), F32),
        compiler_params=_cparams("parallel", "parallel"), name="fnet_slab",
    )(ar, ai, twr, twi, fh, fl)
    return y.reshape(batch * seq_len, ch)


def _attn_kernel(q_ref, k_ref, v_ref, b_ref, o_ref):
    scale = HEAD_DIM ** -0.5
    nq = q_ref.shape[0]
    kw = k_ref.shape[0]
    low = lax.broadcasted_iota(I32, (nq, LANES), 1) < HEAD_DIM
    outs = []
    for p in range(N_HEADS // 2):
        sl = slice(p * LANES, (p + 1) * LANES)
        q2 = q_ref[:, sl]
        zero = jnp.zeros_like(q2)
        qq = jnp.concatenate([jnp.where(low, q2, zero), jnp.where(low, zero, q2)], axis=0)
        s = _dot_nt(qq, k_ref[:, sl]) * scale + b_ref[2 * p:2 * p + 2].reshape(2 * nq, kw)
        m = jnp.max(s, axis=-1, keepdims=True)
        e = jnp.exp(s - m)
        l = jnp.sum(e, axis=-1, keepdims=True)
        o = _dot(e.astype(BF16), v_ref[:, sl]) / l
        outs.append(jnp.where(low, o[:nq], o[nq:]))
    o_ref[...] = jnp.concatenate(outs, axis=-1).astype(o_ref.dtype)


def _attn_bias_table(rpb):
    cols = jnp.arange(GRID_W)
    col_start = jnp.clip(cols - WIN_COLS // 2, 0, GRID_W - WIN_COLS)
    kc = jnp.arange(GRID_W)
    inside = (kc[None, :] >= col_start[:, None]) & (kc[None, :] < col_start[:, None] + WIN_COLS)
    cidx = jnp.clip(kc[None, :] - cols[:, None] + (WIN_COLS - 1), 0, 2 * WIN_COLS - 2)
    dd = jnp.arange(WIN_ROWS)
    ridx = (WIN_ROWS - 1) - dd[:, None] + jnp.arange(WIN_ROWS)[None, :]
    tab = rpb.astype(F32)[:, ridx][:, :, :, cidx]
    tab = jnp.where(inside[None, None, None], tab, NEG_BIG)
    tab = tab.transpose(1, 0, 3, 2, 4)
    return tab.reshape(WIN_ROWS, N_HEADS, GRID_W, WIN_ROWS * GRID_W)


def _attention(q, k, v, bias_tab, tok_off, batch, seq_len):
    rows = seq_len // GRID_W
    assert rows >= WIN_ROWS
    d = q.shape[1]
    kw = WIN_ROWS * GRID_W

    def rs_of(r):
        return jnp.clip(r - WIN_ROWS // 2, 0, rows - WIN_ROWS)

    kspec = pl.BlockSpec((pl.Element(kw), pl.Element(d)),
                         lambda b, r: (pl.multiple_of(tok_off + b * seq_len + rs_of(r) * GRID_W, GRID_W), 0))
    return pl.pallas_call(
        _attn_kernel, grid=(batch, rows),
        in_specs=[pl.BlockSpec((GRID_W, d), lambda b, r: (tok_off // GRID_W + b * rows + r, 0)),
                  kspec, kspec,
                  pl.BlockSpec((None, N_HEADS, GRID_W, kw), lambda b, r: (r - rs_of(r), 0, 0, 0))],
        out_specs=pl.BlockSpec((GRID_W, d), lambda b, r: (b * rows + r, 0)),
        out_shape=jax.ShapeDtypeStruct((batch * seq_len, d), BF16),
        compiler_params=_cparams("parallel", "parallel"), name="natten",
    )(q, k, v, bias_tab)


def _route(scores, rb):
    sel = scores + rb
    tm = sel.shape[1]
    sub = lax.broadcasted_iota(I32, (GROUP_SIZE, tm), 0)
    ninf = -jnp.inf
    groups = []
    for g in range(N_GROUPS):
        sg = sel[g * GROUP_SIZE:(g + 1) * GROUP_SIZE, :]
        m1 = jnp.max(sg, axis=0, keepdims=True)
        i1 = jnp.min(jnp.where(sg == m1, sub, GROUP_SIZE), axis=0, keepdims=True)
        m2 = jnp.max(jnp.where(sub == i1, ninf, sg), axis=0, keepdims=True)
        groups.append(m1 + m2)
    gs = jnp.concatenate(groups, axis=0)
    gsel = jnp.zeros((N_GROUPS, tm), F32)
    for _ in range(TOPK_GROUPS):
        m = jnp.max(gs, axis=0, keepdims=True)
        gi = jnp.min(jnp.where(gs == m, sub, N_GROUPS), axis=0, keepdims=True)
        hit = sub == gi
        gsel = jnp.where(hit, 1.0, gsel)
        gs = jnp.where(hit, ninf, gs)
    masked = jnp.concatenate(
        [jnp.where(gsel[g:g + 1, :] > 0.0, sel[g * GROUP_SIZE:(g + 1) * GROUP_SIZE, :], ninf)
         for g in range(N_GROUPS)], axis=0)
    row = lax.broadcasted_iota(I32, (N_EXPERTS, tm), 0)
    ids, ws = [], []
    for _ in range(TOP_K):
        m = jnp.max(masked, axis=0, keepdims=True)
        ii = jnp.min(jnp.where(masked == m, row, N_EXPERTS), axis=0, keepdims=True)
        hit = row == ii
        ws.append(jnp.sum(jnp.where(hit, scores, 0.0), axis=0, keepdims=True))
        ids.append(ii)
        masked = jnp.where(hit, ninf, masked)
    w = jnp.concatenate(ws, axis=0)
    w = w / jnp.sum(w, axis=0, keepdims=True) * ROUTED_SCALE
    return jnp.concatenate(ids, axis=0), w


def _post_mixer_kernel(*refs, n_parts, split_tile):
    ys = refs[:2 * n_parts]
    ws = refs[2 * n_parts:3 * n_parts]
    (xa_ref, xb_ref, gt_ref, g_ref, sc_ref, sh_ref, gtf_ref, rw_ref, rb_ref, s13_ref, s2_ref,
     xs_ref, h_ref, idx_ref, wts_ref, cnt_ref) = refs[3 * n_parts:]
    acc = None
    for p in range(n_parts):
        y = _pair_read(ys[2 * p], ys[2 * p + 1], split_tile)
        d = _dot(y.astype(BF16), ws[p][...])
        acc = d if acc is None else acc + d
    x = _pair_read(xa_ref, xb_ref, split_tile) + gt_ref[...] * acc
    h = _norm_mod(x, g_ref[...], sc_ref[...], sh_ref[...])
    h_ref[...] = h
    hh, hl = _split(h)
    rh, rl = _split(rw_ref[...])
    logits = _dot_nt(rh, hh) + _dot_nt(rl, hh) + _dot_nt(rh, hl)
    ids, w = _route(jax.nn.sigmoid(logits), rb_ref[...])
    idx_ref[...] = ids
    wts_ref[...] = w
    _, member = _membership(ids, ids.shape[1])
    cnt = jnp.sum(member, axis=1, keepdims=True)
    cnt_ref[...] = jnp.broadcast_to(cnt, cnt_ref.shape).astype(I32)
    u = _dot(hh, s13_ref[...])
    hs = _silu(u[:, :D_SHARED]) * u[:, D_SHARED:]
    xs_ref[...] = x + gtf_ref[...] * _dot(hs.astype(BF16), s2_ref[...])


def _post_mixer(parts, weights, x_pair, gt, g, sc, sh, gtf, rw_t, rb, s13, s2, bounds, split_tile, name):
    d = x_pair[0].shape[1]
    nt = bounds[-1]
    t = nt * TM
    seq = lambda i: (_seq_of_tile(i, bounds), 0, 0)
    in_specs, args = [], []
    for pair in parts:
        in_specs += _pair_specs(pair, split_tile)
        args += list(pair)
    for w in weights:
        in_specs.append(pl.BlockSpec(w.shape, lambda i: (0, 0)))
        args.append(w)
    modspec = pl.BlockSpec((None, 1, d), seq)
    full = lambda a: pl.BlockSpec(a.shape, lambda i: (0, 0))
    in_specs += _pair_specs(x_pair, split_tile) + [modspec, full(g), modspec, modspec, modspec,
                                                   full(rw_t), full(rb), full(s13), full(s2)]
    args += [*x_pair, gt, g, sc, sh, gtf, rw_t, rb, s13, s2]
    return pl.pallas_call(
        functools.partial(_post_mixer_kernel, n_parts=len(parts), split_tile=split_tile),
        grid=(nt,),
        in_specs=in_specs,
        out_specs=[pl.BlockSpec((TM, d), lambda i: (i, 0)),
                   pl.BlockSpec((TM, d), lambda i: (i, 0)),
                   pl.BlockSpec((TOP_K, TM), lambda i: (0, i)),
                   pl.BlockSpec((TOP_K, TM), lambda i: (0, i)),
                   pl.BlockSpec((None, N_EXPERTS, LANES), lambda i: (i, 0, 0))],
        out_shape=[jax.ShapeDtypeStruct((t, d), F32), jax.ShapeDtypeStruct((t, d), F32),
                   jax.ShapeDtypeStruct((TOP_K, t), I32), jax.ShapeDtypeStruct((TOP_K, t), F32),
                   jax.ShapeDtypeStruct((nt, N_EXPERTS, LANES), I32)],
        compiler_params=_cparams("parallel"), name=name,
    )(*args)


def _membership(ids, n_tok):
    row = lax.broadcasted_iota(I32, (N_EXPERTS, n_tok), 0)
    m = jnp.zeros((N_EXPERTS, n_tok), F32)
    for k in range(TOP_K):
        m = m + (row == ids[k:k + 1, :]).astype(F32)
    return row, m


def _dispatch_tables(cnt_tile):
    cnt = cnt_tile[:, :, 0]
    nt = cnt.shape[0]
    carry = jnp.cumsum(cnt, axis=0) - cnt
    counts = jnp.sum(cnt, axis=0)
    padded = (counts + EXPERT_CHUNK - 1) // EXPERT_CHUNK * EXPERT_CHUNK
    pends = jnp.cumsum(padded)
    pstarts = pends - padded
    base = (pstarts[None, :] + carry).astype(F32).reshape(nt, N_EXPERTS, 1)
    n_slots = nt * TM * TOP_K + N_EXPERTS * EXPERT_CHUNK
    n_chunks = n_slots // EXPERT_CHUNK
    chunk_row = jnp.arange(n_chunks, dtype=I32) * EXPERT_CHUNK
    chunk_e = jnp.minimum(jnp.sum((pends[None, :] <= chunk_row[:, None]).astype(I32), axis=1), N_EXPERTS - 1)
    n_used = (pends[-1] // EXPERT_CHUNK).astype(I32).reshape(1)
    pad_lo = (pstarts + counts).astype(I32)
    return base, chunk_e, n_used, pad_lo, pends.astype(I32), n_chunks


def _rank_kernel(ids_ref, base_ref, tri_ref, slot_ref):
    ids = ids_ref[...]
    tm = ids.shape[1]
    row, m = _membership(ids, tm)
    prefix = _dot(m.astype(BF16), tri_ref[...])
    pos = base_ref[...] + (prefix - m)
    slots = [jnp.sum(jnp.where(row == ids[k:k + 1, :], pos, 0.0), axis=0, keepdims=True) for k in range(TOP_K)]
    slot_ref[...] = jnp.concatenate(slots, axis=0).astype(I32)


def _rank(ids, base):
    k, t = ids.shape
    tri = jnp.asarray(np.triu(np.ones((TM, TM), np.float32)).astype(ml_dtypes.bfloat16))
    return pl.pallas_call(
        _rank_kernel, grid=(t // TM,),
        in_specs=[pl.BlockSpec((k, TM), lambda i: (0, i)),
                  pl.BlockSpec((None, N_EXPERTS, 1), lambda i: (i, 0, 0)),
                  pl.BlockSpec((TM, TM), lambda i: (0, 0))],
        out_specs=pl.BlockSpec((k, TM), lambda i: (0, i)),
        out_shape=jax.ShapeDtypeStruct((k, t), I32),
        compiler_params=_cparams("parallel"), name="moe_rank",
    )(ids, base, tri)


PUSH_UNROLL = 4


def _push_kernel(lo_ref, hi_ref, nu_ref, slot_ref, h_ref, xs_hbm, zero, sem):
    i = pl.program_id(0)
    tm = h_ref.shape[0]
    n_chunks = xs_hbm.shape[0] // EXPERT_CHUNK

    def body(tb, carry):
        for j in range(PUSH_UNROLL):
            t = tb * PUSH_UNROLL + j
            for k in range(TOP_K):
                pltpu.make_async_copy(h_ref.at[pl.ds(t, 1), :], xs_hbm.at[pl.ds(slot_ref[k, t], 1), :], sem).start()
        return carry

    lax.fori_loop(0, tm // PUSH_UNROLL, body, 0)

    @pl.when(i == 0)
    def _():
        zero[...] = jnp.zeros_like(zero)

        def zstart(r, carry):
            pltpu.make_async_copy(zero.at[pl.ds(0, 1), :], xs_hbm.at[pl.ds(r, 1), :], sem).start()
            return carry

        def zwait(r, carry):
            pltpu.make_async_copy(zero.at[pl.ds(0, 1), :], xs_hbm.at[pl.ds(0, 1), :], sem).wait()
            return carry

        def per_expert(e, carry):
            lax.fori_loop(lo_ref[e], hi_ref[e], zstart, 0)
            lax.fori_loop(lo_ref[e], hi_ref[e], zwait, 0)
            return carry

        lax.fori_loop(0, N_EXPERTS, per_expert, 0)

        def chunk_copy(c):
            row0 = pl.multiple_of(c * EXPERT_CHUNK, EXPERT_CHUNK)
            return pltpu.make_async_copy(zero, xs_hbm.at[pl.ds(row0, EXPERT_CHUNK), :], sem)

        def cstart(c, carry):
            chunk_copy(c).start()
            return carry

        def cwait(c, carry):
            chunk_copy(c).wait()
            return carry

        lax.fori_loop(nu_ref[0], n_chunks, cstart, 0)
        lax.fori_loop(nu_ref[0], n_chunks, cwait, 0)

    for _ in range(TOP_K):
        pltpu.make_async_copy(h_ref, xs_hbm.at[pl.ds(0, tm), :], sem).wait()


def _push(slot_of, h, pad_lo, pad_hi, n_used, n_slots):
    t, d = h.shape
    nt = t // TM
    grid_spec = pltpu.PrefetchScalarGridSpec(
        num_scalar_prefetch=3,
        grid=(nt,),
        in_specs=[pl.BlockSpec((TOP_K, TM), lambda i, lo, hi, nu: (0, i), memory_space=pltpu.SMEM),
                  pl.BlockSpec((TM, d), lambda i, lo, hi, nu: (i, 0))],
        out_specs=pl.BlockSpec(memory_space=pl.ANY),
        scratch_shapes=[pltpu.VMEM((EXPERT_CHUNK, d), F32), pltpu.SemaphoreType.DMA(())],
    )
    return pl.pallas_call(
        _push_kernel, grid_spec=grid_spec,
        out_shape=jax.ShapeDtypeStruct((n_slots, d), F32),
        compiler_params=_cparams("arbitrary"), name="moe_push",
    )(pad_lo, pad_hi, n_used, slot_of, h)


def _expert_kernel(ce_ref, nu_ref, x_ref, w1_ref, w3_ref, w2_ref, o_ref, w1b, w3b, w2b):
    c = pl.program_id(0)
    live = c < nu_ref[0]

    @pl.when(jnp.logical_and(live, jnp.logical_or(c == 0, ce_ref[c] != ce_ref[jnp.maximum(c - 1, 0)])))
    def _():
        w1b[...] = w1_ref[...].astype(BF16)
        w3b[...] = w3_ref[...].astype(BF16)
        w2b[...] = w2_ref[...].astype(BF16)

    @pl.when(live)
    def _():
        x = x_ref[...].astype(BF16)
        a = _dot(x, w1b[...])
        b = _dot(x, w3b[...])
        o_ref[...] = _dot((_silu(a) * b).astype(BF16), w2b[...])

    @pl.when(c >= nu_ref[0])
    def _():
        o_ref[...] = jnp.zeros_like(o_ref)


def _expert_ffn(xs, chunk_e, n_used, w1, w3, w2, layer):
    n_slots, d = xs.shape
    n_chunks = n_slots // EXPERT_CHUNK
    live = lambda c, nu: jnp.minimum(c, nu[0] - 1)
    wmap = lambda c, ce, nu: (layer, ce[live(c, nu)], 0, 0)
    grid_spec = pltpu.PrefetchScalarGridSpec(
        num_scalar_prefetch=2,
        grid=(n_chunks,),
        in_specs=[
            pl.BlockSpec((EXPERT_CHUNK, d), lambda c, ce, nu: (live(c, nu), 0)),
            pl.BlockSpec((None, None, d, D_EXPERT), wmap),
            pl.BlockSpec((None, None, d, D_EXPERT), wmap),
            pl.BlockSpec((None, None, D_EXPERT, d), wmap),
        ],
        out_specs=pl.BlockSpec((EXPERT_CHUNK, d), lambda c, ce, nu: (c, 0)),
        scratch_shapes=[pltpu.VMEM((d, D_EXPERT), BF16), pltpu.VMEM((d, D_EXPERT), BF16),
                        pltpu.VMEM((D_EXPERT, d), BF16)],
    )
    return pl.pallas_call(
        _expert_kernel, grid_spec=grid_spec,
        out_shape=jax.ShapeDtypeStruct((n_slots, d), F32),
        compiler_params=_cparams("arbitrary"), name="moe_experts",
    )(chunk_e, n_used, xs, w1, w3, w2)


def _combine_kernel(sc_ref, sn_ref, w_ref, xs_ref, gt_ref, ys_hbm, o_ref, buf, sem):
    i = pl.program_id(0)
    n = pl.num_programs(0)
    slot = i % 2

    def issue(s_ref, s):
        def body(rb, carry):
            for j in range(SUBLANES):
                r = rb * SUBLANES + j
                for k in range(TOP_K):
                    pltpu.make_async_copy(ys_hbm.at[pl.ds(s_ref[k, r], 1), :],
                                          buf.at[s, k, pl.ds(r, 1), :], sem.at[s]).start()
            return carry
        lax.fori_loop(0, TC // SUBLANES, body, 0)

    @pl.when(i == 0)
    def _():
        issue(sc_ref, 0)

    @pl.when(i + 1 < n)
    def _():
        issue(sn_ref, 1 - slot)

    for k in range(TOP_K):
        pltpu.make_async_copy(ys_hbm.at[pl.ds(0, TC), :], buf.at[slot, k], sem.at[slot]).wait()
    d = o_ref.shape[1]
    acc = None
    for k in range(TOP_K):
        wcol = jnp.transpose(jnp.broadcast_to(w_ref[k:k + 1, :], (LANES, TC)))
        term = buf[slot, k] * _lane_tile(wcol, d)
        acc = term if acc is None else acc + term
    o_ref[...] = xs_ref[...] + gt_ref[...] * acc


def _combine(slot_of, wts, xs, gtf, ys, bounds_tc):
    t, d = xs.shape
    nt = t // TC
    seq = lambda i: (_seq_of_tile(i, bounds_tc), 0, 0)
    return pl.pallas_call(
        _combine_kernel, grid=(nt,),
        in_specs=[pl.BlockSpec((TOP_K, TC), lambda i: (0, i), memory_space=pltpu.SMEM),
                  pl.BlockSpec((TOP_K, TC), lambda i: (0, jnp.minimum(i + 1, nt - 1)), memory_space=pltpu.SMEM),
                  pl.BlockSpec((TOP_K, TC), lambda i: (0, i)),
                  pl.BlockSpec((TC, d), lambda i: (i, 0)),
                  pl.BlockSpec((None, 1, d), seq),
                  pl.BlockSpec(memory_space=pl.ANY)],
        out_specs=pl.BlockSpec((TC, d), lambda i: (i, 0)),
        out_shape=jax.ShapeDtypeStruct((t, d), F32),
        scratch_shapes=[pltpu.VMEM((2, TOP_K, TC, d), F32), pltpu.SemaphoreType.DMA((2,))],
        compiler_params=_cparams("arbitrary"), name="moe_combine",
    )(slot_of, slot_of, wts, xs, gtf, ys)


def _final_norm_kernel(x_ref, g_ref, o_ref):
    x = x_ref[...]
    ms = jnp.mean(x * x, axis=-1, keepdims=True)
    o_ref[...] = x * lax.rsqrt(ms + EPS) * g_ref[...]


def _final_norm(x, g, tile_off, n_tiles):
    d = x.shape[1]
    return pl.pallas_call(
        _final_norm_kernel, grid=(n_tiles,),
        in_specs=[pl.BlockSpec((TM, d), lambda i: (i + tile_off, 0)), pl.BlockSpec((1, d), lambda i: (0, 0))],
        out_specs=pl.BlockSpec((TM, d), lambda i: (i, 0)),
        out_shape=jax.ShapeDtypeStruct((n_tiles * TM, d), F32),
        compiler_params=_cparams("parallel"), name="final_norm",
    )(x, g)


def kernel(x_prompt, x_sample, c_prompt, c_sample, ada_w, ada_b, g_mix, g_ffn, g_final,
           w_in_ab, w_out_ab, conv_w, conv_b, filt_w1, filt_b1, filt_f1, filt_w2, filt_b2, filt_f2,
           filt_w3, decay, hyena_bias, w_qkv, w_out_c, rpb,
           router_w, router_b, exp_w1, exp_w3, exp_w2, sh_w1, sh_w3, sh_w2):
    d = D_MODEL
    bp, lp, _ = x_prompt.shape
    bs, ls, _ = x_sample.shape
    trunks = ((0, bp, lp), (bp * lp, bs, ls))
    t_all = bp * lp + bs * ls
    n_seq = bp + bs
    depth = ada_w.shape[0]
    assert lp % TM == 0 and ls % TM == 0
    bounds, acc = [], 0
    for _, b, l in trunks:
        for _ in range(b):
            acc += l // TM
            bounds.append(acc)
    bounds = tuple(bounds)
    bounds_tc = tuple(b * (TM // TC) for b in bounds)
    split_tile = bp * lp // TM

    x_pair = (x_prompt.reshape(bp * lp, d), x_sample.reshape(bs * ls, d))
    c_all = jnp.concatenate([c_prompt, c_sample, jnp.zeros((SUBLANES - n_seq, d), F32)], axis=0)
    mod = _modulation(c_all, ada_w, ada_b)[:, :n_seq]
    mod = mod.reshape(depth, n_seq, 6, 1, d)

    bands = jnp.linspace(1e-4, POS_BANDS - 1, POS_BANDS, dtype=F32).reshape(1, POS_BANDS)
    bias_tab = None

    for l in range(depth):
        sh_m, sc_m, gt_m, sh_f, sc_f, gt_f = (mod[l, :, j] for j in range(6))
        g_m = g_mix[l].reshape(1, d)
        i = l // 2
        if l % 2 == 0:
            u_f, u_h = _norm_mod_matmul(x_pair, g_m, sc_m, sh_m, w_in_ab[i].astype(BF16),
                                        (D_FNET, 3 * D_HYENA), (F32, F32), bounds, split_tile, "in_proj_ab")
            z, x0 = _hyena_pre(u_h, conv_w[i], conv_b[i], bounds)
            ya, yb = [], []
            for tok_off, b, sl in trunks:
                plan = _HyenaPlan(sl)
                hc, ss = _hyena_filter(sl, bands, filt_w1[i], filt_b1[i], filt_f1[i], filt_w2[i],
                                       filt_b2[i], filt_f2[i], filt_w3[i], decay[i])
                hr, hi = _hyena_spectrum(plan, hc)
                yb.append(_hyena_longconv(plan, z, x0, hyena_bias[i], hr, hi, ss, tok_off, b, sl))
                ya.append(_fnet_mix(u_f, tok_off, b, sl))
            parts = [tuple(ya), tuple(yb)]
            w_out = w_out_ab[i].astype(BF16)
            weights = [w_out[:D_FNET], w_out[D_FNET:]]
        else:
            q, k, v = _norm_mod_matmul(x_pair, g_m, sc_m, sh_m, w_qkv[i].astype(BF16),
                                       (d, d, d), (BF16, BF16, BF16), bounds, split_tile, "qkv_proj")
            bias_tab = _attn_bias_table(rpb[i])
            o = tuple(_attention(q, k, v, bias_tab, tok_off, b, sl) for tok_off, b, sl in trunks)
            parts = [o]
            weights = [w_out_c[i].astype(BF16)]
        s13 = jnp.concatenate([sh_w1[l], sh_w3[l]], axis=1).astype(BF16)
        xs, h, idx, wts, cnt_tile = _post_mixer(
            parts, weights, x_pair, gt_m, g_ffn[l].reshape(1, d), sc_f, sh_f, gt_f,
            router_w[l].T, router_b[l].reshape(N_EXPERTS, 1), s13, sh_w2[l].astype(BF16),
            bounds, split_tile, "post_mixer_%d" % l)
        base, chunk_e, n_used, pad_lo, pad_hi, n_chunks = _dispatch_tables(cnt_tile)
        slot_of = _rank(idx, base)
        xg = _push(slot_of, h, pad_lo, pad_hi, n_used, n_chunks * EXPERT_CHUNK)
        ys = _expert_ffn(xg, chunk_e, n_used, exp_w1, exp_w3, exp_w2, l)
        x = _combine(slot_of, wts, xs, gt_f, ys, bounds_tc)
        x_pair = (x, x)

    g_fin = g_final.reshape(1, d)
    y_prompt = _final_norm(x, g_fin, 0, bp * lp // TM).reshape(bp, lp, d)
    y_sample = _final_norm(x, g_fin, bp * lp // TM, bs * ls // TM).reshape(bs, ls, d)
    return (y_prompt, y_sample)
```

```python
import functools
import math

import ml_dtypes
import numpy as np

import jax
import jax.numpy as jnp
from jax import lax
from jax.experimental import pallas as pl
from jax.experimental.pallas import tpu as pltpu

F32 = jnp.float32
BF16 = jnp.bfloat16
I32 = jnp.int32
U32 = jnp.uint32

D_MODEL = 1024
GRID_W = 64
D_FNET = 512
N_FNET_GROUPS = 4
D_GROUP = D_FNET // N_FNET_GROUPS
D_HYENA = 512
POS_BANDS = 16
N_HEADS = 16
HEAD_DIM = 64
WIN_ROWS = 8
WIN_COLS = 16
N_EXPERTS = 64
TOP_K = 8
N_GROUPS = 8
GROUP_SIZE = N_EXPERTS // N_GROUPS
TOPK_GROUPS = 4
D_EXPERT = 256
D_SHARED = 256
ROUTED_SCALE = 2.5
EXPERT_CHUNK = 256
EPS = 1e-6
NEG_BIG = -1e30

LANES = 128
SUBLANES = 8
TM = 512
TC = 128
NB = 8
VMEM_LIMIT = 56 * 1024 * 1024


def _cparams(*sem):
    return pltpu.CompilerParams(dimension_semantics=sem, vmem_limit_bytes=VMEM_LIMIT)


def _dot(a, b):
    return jnp.dot(a, b, preferred_element_type=F32)


def _dot_nt(a, b):
    return lax.dot_general(a, b, (((1,), (1,)), ((), ())), preferred_element_type=F32)


def _split(x):
    hi = x.astype(BF16)
    lo = (x - hi.astype(F32)).astype(BF16)
    return hi, lo


def _dot3_lc(mh, ml, x):
    xh, xl = _split(x)
    return _dot(mh, xh) + _dot(mh, xl) + _dot(ml, xh)


def _dot3_rc(x, mh, ml):
    xh, xl = _split(x)
    return _dot(xh, mh) + _dot(xl, mh) + _dot(xh, ml)


def _dot3(a, b):
    ah, al = _split(a)
    bh, bl = _split(b)
    return _dot(ah, bh) + _dot(al, bh) + _dot(ah, bl)


def _silu(x):
    return x * jax.nn.sigmoid(x)


def _lane_tile(t, width):
    return jnp.concatenate([t] * (width // LANES), axis=-1)


def _pack_rows(x):
    n = x.shape[1] // 2
    xr = x.astype(BF16).astype(F32)
    hi = lax.bitcast_convert_type(xr[:, :n], U32)
    lo = lax.bitcast_convert_type(xr[:, n:], U32)
    return hi | (lo >> 16)


def _unpack_rows(p):
    hi = lax.bitcast_convert_type(p & jnp.uint32(0xFFFF0000), F32)
    lo = lax.bitcast_convert_type(p << 16, F32)
    return hi, lo


def _cmul(ar, ai, br, bi):
    return ar * br - ai * bi, ar * bi + ai * br


def _norm_mod(x, g, sc, sh):
    ms = jnp.mean(x * x, axis=-1, keepdims=True)
    return (x * lax.rsqrt(ms + EPS) * g) * (1.0 + sc) + sh


def _seq_of_tile(i, bounds):
    s = jnp.int32(0)
    for b in bounds[:-1]:
        s = s + (i >= b).astype(I32)
    return s


def _hilo(m):
    m = np.asarray(m, np.float64)
    hi = m.astype(np.float32).astype(ml_dtypes.bfloat16)
    lo = (m - hi.astype(np.float64)).astype(np.float32).astype(ml_dtypes.bfloat16)
    return jnp.asarray(hi), jnp.asarray(lo)


def _cos_sin(n):
    j = np.arange(n)
    ang = 2.0 * np.pi * ((j[:, None] * j[None, :]) % n) / n
    return np.cos(ang), np.sin(ang)


def _twiddle(n_slab, n_row, n_total):
    s = jnp.arange(n_slab, dtype=I32)[:, None]
    r = jnp.arange(n_row, dtype=I32)[None, :]
    ang = ((s * r) % n_total).astype(F32) * (2.0 * math.pi / n_total)
    shape = (n_slab, n_row, LANES)
    return (jnp.broadcast_to(jnp.cos(ang)[:, :, None], shape),
            jnp.broadcast_to(-jnp.sin(ang)[:, :, None], shape))


def _mod_kernel(c_ref, w_ref, b_ref, o_ref):
    o_ref[...] = _dot3(_silu(c_ref[...]), w_ref[...]) + b_ref[...]


def _modulation(c_pad, ada_w, ada_b):
    depth, d, n = ada_w.shape
    tn = 1536
    return pl.pallas_call(
        _mod_kernel,
        grid=(depth, n // tn),
        in_specs=[pl.BlockSpec((SUBLANES, d), lambda l, j: (0, 0)),
                  pl.BlockSpec((None, d, tn), lambda l, j: (l, 0, j)),
                  pl.BlockSpec((None, 1, tn), lambda l, j: (l, 0, j))],
        out_specs=pl.BlockSpec((None, SUBLANES, tn), lambda l, j: (l, 0, j)),
        out_shape=jax.ShapeDtypeStruct((depth, SUBLANES, n), F32),
        compiler_params=_cparams("parallel", "parallel"),
        name="adaln_mod",
    )(c_pad, ada_w, ada_b.reshape(depth, 1, n))


def _pair_specs(pair, split_tile):
    a, b = pair
    w = a.shape[1]
    b_off = 0 if a is b else split_tile
    return [pl.BlockSpec((TM, w), lambda i: (jnp.minimum(i, split_tile - 1), 0)),
            pl.BlockSpec((TM, w), lambda i: (jnp.maximum(i, split_tile) - b_off, 0))]


def _pair_read(a_ref, b_ref, split_tile):
    return jnp.where(pl.program_id(0) < split_tile, a_ref[...], b_ref[...])


def _nmm_kernel(xa_ref, xb_ref, g_ref, sc_ref, sh_ref, w_ref, *o_refs, split_tile):
    x = _pair_read(xa_ref, xb_ref, split_tile)
    h = _norm_mod(x, g_ref[...], sc_ref[...], sh_ref[...])
    u = _dot(h.astype(BF16), w_ref[...])
    off = 0
    for o in o_refs:
        n = o.shape[-1]
        o[...] = u[:, off:off + n].astype(o.dtype)
        off += n


def _norm_mod_matmul(x_pair, g, sc, sh, w_bf16, splits, out_dtypes, bounds, split_tile, name):
    d, n = w_bf16.shape
    nt = bounds[-1]
    seq = lambda i: (_seq_of_tile(i, bounds), 0, 0)
    return pl.pallas_call(
        functools.partial(_nmm_kernel, split_tile=split_tile),
        grid=(nt,),
        in_specs=_pair_specs(x_pair, split_tile) + [
            pl.BlockSpec((1, d), lambda i: (0, 0)),
            pl.BlockSpec((None, 1, d), seq),
            pl.BlockSpec((None, 1, d), seq),
            pl.BlockSpec((d, n), lambda i: (0, 0))],
        out_specs=[pl.BlockSpec((TM, s), lambda i: (i, 0)) for s in splits],
        out_shape=[jax.ShapeDtypeStruct((nt * TM, s), dt) for s, dt in zip(splits, out_dtypes)],
        compiler_params=_cparams("parallel"),
        name=name,
    )(*x_pair, g, sc, sh, w_bf16)


def _hyena_pre_kernel(prev_ref, cur_ref, next_ref, cw_ref, cb_ref, z_ref, x0_ref, *, first_tiles, last_tiles):
    i = pl.program_id(0)
    cur = cur_ref[...]
    tm = cur.shape[0]
    is_first = functools.reduce(jnp.logical_or, [i == f for f in first_tiles])
    is_last = functools.reduce(jnp.logical_or, [i == f for f in last_tiles])
    prow = jnp.where(is_first, 0.0, prev_ref[SUBLANES - 1:SUBLANES, :])
    nrow = jnp.where(is_last, 0.0, next_ref[0:1, :])
    rid = lax.broadcasted_iota(I32, (tm, 1), 0)
    up = jnp.where(rid == 0, prow, pltpu.roll(cur, 1, 0))
    dn = jnp.where(rid == tm - 1, nrow, pltpu.roll(cur, tm - 1, 0))
    cw = cw_ref[...]
    uc = cb_ref[...] + (up * cw[0:1, :] + cur * cw[1:2, :] + dn * cw[2:3, :])
    x0_ref[...] = uc[:, :D_HYENA]
    z_ref[...] = uc[:, 2 * D_HYENA:] * uc[:, D_HYENA:2 * D_HYENA]


def _hyena_pre(u_h, conv_w, conv_b, bounds):
    t, c = u_h.shape
    nt = t // TM
    rb = TM // SUBLANES
    first_tiles = (0,) + tuple(bounds[:-1])
    last_tiles = tuple(b - 1 for b in bounds)
    kern = functools.partial(_hyena_pre_kernel, first_tiles=first_tiles, last_tiles=last_tiles)
    return pl.pallas_call(
        kern,
        grid=(nt,),
        in_specs=[pl.BlockSpec((SUBLANES, c), lambda i: (jnp.maximum(i * rb - 1, 0), 0)),
                  pl.BlockSpec((TM, c), lambda i: (i, 0)),
                  pl.BlockSpec((SUBLANES, c), lambda i: (jnp.minimum((i + 1) * rb, nt * rb - 1), 0)),
                  pl.BlockSpec((3, c), lambda i: (0, 0)),
                  pl.BlockSpec((1, c), lambda i: (0, 0))],
        out_specs=[pl.BlockSpec((TM, D_HYENA), lambda i: (i, 0)),
                   pl.BlockSpec((TM, D_HYENA), lambda i: (i, 0))],
        out_shape=[jax.ShapeDtypeStruct((t, D_HYENA), F32), jax.ShapeDtypeStruct((t, D_HYENA), F32)],
        compiler_params=_cparams("parallel"),
        name="hyena_pre",
    )(u_h, u_h, u_h, conv_w, conv_b.reshape(1, c))


def _filter_kernel(bands_ref, w1_ref, b1_ref, f1_ref, w2_ref, b2_ref, f2_ref, w3_ref, dec_ref,
                   hc_ref, ss_ref, *, seq_len):
    i = pl.program_id(0)
    tm = hc_ref.shape[0]
    row = i * tm + lax.broadcasted_iota(I32, (tm, 1), 0)
    is_bwd = i >= seq_len // tm
    n = jnp.where(is_bwd, 2 * seq_len - row, row).astype(F32)
    t = n / float(seq_len - 1)
    ang = (2.0 * math.pi / seq_len) * n * bands_ref[...]
    w1 = w1_ref[...]
    pre = (t * w1[0:1, :] + _dot3(jnp.cos(ang), w1[1:1 + POS_BANDS, :])
           + _dot3(-jnp.sin(ang), w1[1 + POS_BANDS:, :]) + b1_ref[...])
    h = jnp.sin(f1_ref[...] * pre)
    h = jnp.sin(f2_ref[...] * (_dot3(h, w2_ref[...]) + b2_ref[...]))
    w3 = w3_ref[...]
    dec = jnp.abs(dec_ref[...])
    h = _dot3(h, jnp.where(is_bwd, w3[:, D_HYENA:], w3[:, :D_HYENA]))
    hc = h * jnp.exp(-t * jnp.where(is_bwd, dec[1:2, :], dec[0:1, :]))
    hc = jnp.where(row == seq_len, 0.0, hc)
    hc_ref[...] = hc

    @pl.when(i == 0)
    def _():
        ss_ref[...] = jnp.zeros_like(ss_ref)

    ss_ref[...] += jnp.sum(hc * hc, axis=0, keepdims=True)


def _hyena_filter(seq_len, bands, w1, b1, f1, w2, b2, f2, w3, decay):
    tm = 512
    full = lambda a: pl.BlockSpec(a.shape, lambda i: (0,) * a.ndim)
    args = (bands, w1, b1.reshape(1, -1), f1.reshape(1, -1), w2, b2.reshape(1, -1), f2.reshape(1, -1), w3, decay)
    return pl.pallas_call(
        functools.partial(_filter_kernel, seq_len=seq_len),
        grid=(2 * seq_len // tm,),
        in_specs=[full(a) for a in args],
        out_specs=[pl.BlockSpec((tm, D_HYENA), lambda i: (i, 0)),
                   pl.BlockSpec((1, D_HYENA), lambda i: (0, 0))],
        out_shape=[jax.ShapeDtypeStruct((2 * seq_len, D_HYENA), F32),
                   jax.ShapeDtypeStruct((1, D_HYENA), F32)],
        compiler_params=_cparams("arbitrary"),
        name="hyena_filter",
    )(*args)


def _strided_fwd_kernel(x_ref, mh_ref, ml_ref, or_ref, oi_ref):
    n1 = or_ref.shape[0]
    mh = mh_ref[...]
    ml = ml_ref[...]
    for j in range(NB):
        a = _dot3_lc(mh, ml, x_ref[:, j, :])
        or_ref[:, j, :] = a[:n1]
        oi_ref[:, j, :] = a[n1:]


def _strided_fwd(x3, row_block, rows_in, m, batch, name):
    _, n2, ch = x3.shape
    mh, ml = m
    n1 = mh.shape[0] // 2
    out = jax.ShapeDtypeStruct((batch, n1, n2, ch), F32)
    ospec = pl.BlockSpec((None, n1, NB, ch), lambda b, j: (b, 0, j, 0))
    return pl.pallas_call(
        _strided_fwd_kernel,
        grid=(batch, n2 // NB),
        in_specs=[pl.BlockSpec((rows_in, NB, ch), lambda b, j: (row_block + b, j, 0)),
                  pl.BlockSpec(mh.shape, lambda b, j: (0, 0)),
                  pl.BlockSpec(ml.shape, lambda b, j: (0, 0))],
        out_specs=[ospec, ospec],
        out_shape=[out, out],
        compiler_params=_cparams("parallel", "parallel"),
        name=name,
    )(x3, mh, ml)


def _slab_specs(n1, n2, ch, batched):
    if batched:
        return pl.BlockSpec((None, NB, n2, ch), lambda b, k: (b, k, 0, 0))
    return pl.BlockSpec((None, NB, n2, ch), lambda b, k: (0, k, 0, 0))


def _slab_fwd_kernel(ar_ref, ai_ref, twr_ref, twi_ref, fh_ref, fl_ref, or_ref, oi_ref):
    n2, ch = ar_ref.shape[1:]
    fh = fh_ref[...]
    fl = fl_ref[...]
    for s in range(NB):
        twr = _lane_tile(twr_ref[s], ch)
        twi = _lane_tile(twi_ref[s], ch)
        br, bi = _cmul(ar_ref[s], ai_ref[s], twr, twi)
        x = _dot3_lc(fh, fl, jnp.concatenate([br, bi], axis=0))
        or_ref[s] = x[:n2]
        oi_ref[s] = x[n2:]


def _slab_conv_kernel(ar_ref, ai_ref, hr_ref, hi_ref, ss_ref, twr_ref, twi_ref,
                      fh_ref, fl_ref, gh_ref, gl_ref, or_ref, oi_ref):
    n2, ch = ar_ref.shape[1:]
    fh = fh_ref[...]
    fl = fl_ref[...]
    gh = gh_ref[...]
    gl = gl_ref[...]
    scale = lax.rsqrt(ss_ref[...] + EPS)
    for s in range(NB):
        twr = _lane_tile(twr_ref[s], ch)
        twi = _lane_tile(twi_ref[s], ch)
        br, bi = _cmul(ar_ref[s], ai_ref[s], twr, twi)
        x = _dot3_lc(fh, fl, jnp.concatenate([br, bi], axis=0))
        pr, pi = _cmul(x[:n2], x[n2:], hr_ref[s] * scale, hi_ref[s] * scale)
        g = _dot3_lc(gh, gl, jnp.concatenate([pr, pi], axis=0))
        qr, qi = _cmul(g[:n2], g[n2:], twr, -twi)
        or_ref[s] = qr
        oi_ref[s] = qi


def _hyena_out_kernel(gr_ref, gi_ref, z_ref, x0_ref, hb_ref, mh_ref, ml_ref, o_ref):
    mh = mh_ref[...]
    ml = ml_ref[...]
    hb = hb_ref[...]
    for j in range(NB):
        g = jnp.concatenate([gr_ref[:, j, :], gi_ref[:, j, :]], axis=0)
        zc = _dot3_lc(mh, ml, g)
        o_ref[:, j, :] = x0_ref[:, j, :] * (zc + z_ref[:, j, :] * hb)


class _HyenaPlan:
    def __init__(self, seq_len):
        n = 2 * seq_len
        self.n = n
        self.n1 = 256 if n >= 32768 else 128
        self.n2 = n // self.n1
        self.half = self.n1 // 2
        c1, s1 = _cos_sin(self.n1)
        c2, s2 = _cos_sin(self.n2)
        h = self.half
        self.m1 = _hilo(np.concatenate([c1, -s1], axis=0))
        self.m1_half = _hilo(np.concatenate([c1[:, :h], -s1[:, :h]], axis=0))
        self.f2 = _hilo(np.block([[c2, s2], [-s2, c2]]))
        self.g2 = _hilo(np.block([[c2, -s2], [s2, c2]]) / n)
        self.m2 = _hilo(np.concatenate([c1[:h], -s1[:h]], axis=1))


def _hyena_spectrum(plan, hc):
    n1, n2 = plan.n1, plan.n2
    ch = hc.shape[1]
    ar, ai = _strided_fwd(hc.reshape(n1, n2, ch), 0, n1, plan.m1, 1, "hyena_filt_s1")
    twr, twi = _twiddle(n1, n2, plan.n)
    fh, fl = plan.f2
    spec = _slab_specs(n1, n2, ch, True)
    tspec = pl.BlockSpec((NB, n2, LANES), lambda b, k: (k, 0, 0))
    cspec = pl.BlockSpec(fh.shape, lambda b, k: (0, 0))
    out = jax.ShapeDtypeStruct((1, n1, n2, ch), F32)
    return pl.pallas_call(
        _slab_fwd_kernel, grid=(1, n1 // NB),
        in_specs=[spec, spec, tspec, tspec, cspec, cspec],
        out_specs=[spec, spec], out_shape=[out, out],
        compiler_params=_cparams("parallel", "parallel"), name="hyena_filt_slab",
    )(ar, ai, twr, twi, fh, fl)


def _hyena_longconv(plan, z, x0, hyena_bias, hr, hi, ss, tok_off, batch, seq_len):
    n1, n2, half = plan.n1, plan.n2, plan.half
    t, ch = z.shape
    assert tok_off % seq_len == 0 and seq_len == half * n2
    row_block = tok_off // seq_len
    z3 = z.reshape(t // n2, n2, ch)
    x03 = x0.reshape(t // n2, n2, ch)
    ar, ai = _strided_fwd(z3, row_block, half, plan.m1_half, batch, "hyena_s1")
    twr, twi = _twiddle(n1, n2, plan.n)
    fh, fl = plan.f2
    gh, gl = plan.g2
    spec = _slab_specs(n1, n2, ch, True)
    hspec = _slab_specs(n1, n2, ch, False)
    tspec = pl.BlockSpec((NB, n2, LANES), lambda b, k: (k, 0, 0))
    cspec = pl.BlockSpec(fh.shape, lambda b, k: (0, 0))
    out = jax.ShapeDtypeStruct((batch, n1, n2, ch), F32)
    gr, gi = pl.pallas_call(
        _slab_conv_kernel, grid=(batch, n1 // NB),
        in_specs=[spec, spec, hspec, hspec, pl.BlockSpec((1, ch), lambda b, k: (0, 0)),
                  tspec, tspec, cspec, cspec, cspec, cspec],
        out_specs=[spec, spec], out_shape=[out, out],
        compiler_params=_cparams("parallel", "parallel"), name="hyena_slab_conv",
    )(ar, ai, hr, hi, ss, twr, twi, fh, fl, gh, gl)
    mh, ml = plan.m2
    gspec = pl.BlockSpec((None, n1, NB, ch), lambda b, j: (b, 0, j, 0))
    xspec = pl.BlockSpec((half, NB, ch), lambda b, j: (row_block + b, j, 0))
    y = pl.pallas_call(
        _hyena_out_kernel, grid=(batch, n2 // NB),
        in_specs=[gspec, gspec, xspec, xspec, pl.BlockSpec((1, ch), lambda b, j: (0, 0)),
                  pl.BlockSpec(mh.shape, lambda b, j: (0, 0)), pl.BlockSpec(ml.shape, lambda b, j: (0, 0))],
        out_specs=pl.BlockSpec((half, NB, ch), lambda b, j: (b, j, 0)),
        out_shape=jax.ShapeDtypeStruct((batch * half, n2, ch), F32),
        compiler_params=_cparams("parallel", "parallel"), name="hyena_s2",
    )(gr, gi, z3, x03, hyena_bias.reshape(1, ch), mh, ml)
    return y.reshape(batch * seq_len, ch)


def _fnet_s1_kernel(x_ref, wh_ref, wl_ref, mh_ref, ml_ref, or_ref, oi_ref):
    n1 = or_ref.shape[0]
    wh = wh_ref[...]
    wl = wl_ref[...]
    mh = mh_ref[...]
    ml = ml_ref[...]
    for j in range(NB):
        x = x_ref[:, j, :]
        zr, zi = [], []
        for g in range(N_FNET_GROUPS):
            zg = _dot3_rc(x[:, g * D_GROUP:(g + 1) * D_GROUP], wh, wl)
            zr.append(zg[:, :D_GROUP])
            zi.append(zg[:, D_GROUP:])
        z = jnp.concatenate([jnp.concatenate(zr, axis=1), jnp.concatenate(zi, axis=1)], axis=0)
        a = _dot3_lc(mh, ml, z)
        or_ref[:, j, :] = a[:n1]
        oi_ref[:, j, :] = a[n1:]


def _fnet_slab_kernel(ar_ref, ai_ref, twr_ref, twi_ref, fh_ref, fl_ref, o_ref, *, scale):
    ch = ar_ref.shape[2]
    fh = fh_ref[...]
    fl = fl_ref[...]
    for s in range(NB):
        twr = _lane_tile(twr_ref[s], ch)
        twi = _lane_tile(twi_ref[s], ch)
        br, bi = _cmul(ar_ref[s], ai_ref[s], twr, twi)
        o_ref[:, s, :] = _dot3_lc(fh, fl, jnp.concatenate([br, bi], axis=0)) * scale


def _fnet_mix(u_f, tok_off, batch, seq_len):
    t, ch = u_f.shape
    n2 = 128
    n1 = seq_len // n2
    assert tok_off % seq_len == 0
    row_block = tok_off // seq_len
    c1, s1 = _cos_sin(n1)
    c2, s2 = _cos_sin(n2)
    cg, sg = _cos_sin(D_GROUP)
    wh, wl = _hilo(np.concatenate([cg, -sg], axis=1))
    mh, ml = _hilo(np.block([[c1, s1], [-s1, c1]]))
    fh, fl = _hilo(np.concatenate([c2, s2], axis=1))
    u3 = u_f.reshape(t // n2, n2, ch)
    out = jax.ShapeDtypeStruct((batch, n1, n2, ch), F32)
    ospec = pl.BlockSpec((None, n1, NB, ch), lambda b, j: (b, 0, j, 0))
    c2d = lambda a: pl.BlockSpec(a.shape, lambda b, j: (0, 0))
    ar, ai = pl.pallas_call(
        _fnet_s1_kernel, grid=(batch, n2 // NB),
        in_specs=[pl.BlockSpec((n1, NB, ch), lambda b, j: (row_block + b, j, 0)),
                  c2d(wh), c2d(wl), c2d(mh), c2d(ml)],
        out_specs=[ospec, ospec], out_shape=[out, out],
        compiler_params=_cparams("parallel", "parallel"), name="fnet_s1",
    )(u3, wh, wl, mh, ml)
    twr, twi = _twiddle(n1, n2, seq_len)
    spec = _slab_specs(n1, n2, ch, True)
    tspec = pl.BlockSpec((NB, n2, LANES), lambda b, k: (k, 0, 0))
    scale = 1.0 / math.sqrt(seq_len * D_GROUP)
    y = pl.pallas_call(
        functools.partial(_fnet_slab_kernel, scale=scale), grid=(batch, n1 // NB),
        in_specs=[spec, spec, tspec, tspec, c2d(fh), c2d(fl)],
        out_specs=pl.BlockSpec((None, n2, NB, ch), lambda b, k: (b, 0, k, 0)),
        out_shape=jax.ShapeDtypeStruct((batch, n2, n1, ch), F32),
        compiler_params=_cparams("parallel", "parallel"), name="fnet_slab",
    )(ar, ai, twr, twi, fh, fl)
    return y.reshape(batch * seq_len, ch)


def _attn_kernel(q_ref, k_ref, v_ref, b_ref, o_ref):
    scale = HEAD_DIM ** -0.5
    nq = q_ref.shape[0]
    kw = k_ref.shape[0]
    low = lax.broadcasted_iota(I32, (nq, LANES), 1) < HEAD_DIM
    outs = []
    for p in range(N_HEADS // 2):
        sl = slice(p * LANES, (p + 1) * LANES)
        q2 = q_ref[:, sl]
        zero = jnp.zeros_like(q2)
        qq = jnp.concatenate([jnp.where(low, q2, zero), jnp.where(low, zero, q2)], axis=0)
        s = _dot_nt(qq, k_ref[:, sl]) * scale + b_ref[2 * p:2 * p + 2].reshape(2 * nq, kw)
        m = jnp.max(s, axis=-1, keepdims=True)
        e = jnp.exp(s - m)
        l = jnp.sum(e, axis=-1, keepdims=True)
        o = _dot(e.astype(BF16), v_ref[:, sl]) / l
        outs.append(jnp.where(low, o[:nq], o[nq:]))
    o_ref[...] = jnp.concatenate(outs, axis=-1).astype(o_ref.dtype)


def _attn_bias_table(rpb):
    cols = jnp.arange(GRID_W)
    col_start = jnp.clip(cols - WIN_COLS // 2, 0, GRID_W - WIN_COLS)
    kc = jnp.arange(GRID_W)
    inside = (kc[None, :] >= col_start[:, None]) & (kc[None, :] < col_start[:, None] + WIN_COLS)
    cidx = jnp.clip(kc[None, :] - cols[:, None] + (WIN_COLS - 1), 0, 2 * WIN_COLS - 2)
    dd = jnp.arange(WIN_ROWS)
    ridx = (WIN_ROWS - 1) - dd[:, None] + jnp.arange(WIN_ROWS)[None, :]
    tab = rpb.astype(F32)[:, ridx][:, :, :, cidx]
    tab = jnp.where(inside[None, None, None], tab, NEG_BIG)
    tab = tab.transpose(1, 0, 3, 2, 4)
    return tab.reshape(WIN_ROWS, N_HEADS, GRID_W, WIN_ROWS * GRID_W)


def _attention(q, k, v, bias_tab, tok_off, batch, seq_len):
    rows = seq_len // GRID_W
    assert rows >= WIN_ROWS
    d = q.shape[1]
    kw = WIN_ROWS * GRID_W

    def rs_of(r):
        return jnp.clip(r - WIN_ROWS // 2, 0, rows - WIN_ROWS)

    kspec = pl.BlockSpec((pl.Element(kw), pl.Element(d)),
                         lambda b, r: (pl.multiple_of(tok_off + b * seq_len + rs_of(r) * GRID_W, GRID_W), 0))
    return pl.pallas_call(
        _attn_kernel, grid=(batch, rows),
        in_specs=[pl.BlockSpec((GRID_W, d), lambda b, r: (tok_off // GRID_W + b * rows + r, 0)),
                  kspec, kspec,
                  pl.BlockSpec((None, N_HEADS, GRID_W, kw), lambda b, r: (r - rs_of(r), 0, 0, 0))],
        out_specs=pl.BlockSpec((GRID_W, d), lambda b, r: (b * rows + r, 0)),
        out_shape=jax.ShapeDtypeStruct((batch * seq_len, d), BF16),
        compiler_params=_cparams("parallel", "parallel"), name="natten",
    )(q, k, v, bias_tab)


def _route(scores, rb):
    sel = scores + rb
    tm = sel.shape[1]
    sub = lax.broadcasted_iota(I32, (GROUP_SIZE, tm), 0)
    ninf = -jnp.inf
    groups = []
    for g in range(N_GROUPS):
        sg = sel[g * GROUP_SIZE:(g + 1) * GROUP_SIZE, :]
        m1 = jnp.max(sg, axis=0, keepdims=True)
        i1 = jnp.min(jnp.where(sg == m1, sub, GROUP_SIZE), axis=0, keepdims=True)
        m2 = jnp.max(jnp.where(sub == i1, ninf, sg), axis=0, keepdims=True)
        groups.append(m1 + m2)
    gs = jnp.concatenate(groups, axis=0)
    gsel = jnp.zeros((N_GROUPS, tm), F32)
    for _ in range(TOPK_GROUPS):
        m = jnp.max(gs, axis=0, keepdims=True)
        gi = jnp.min(jnp.where(gs == m, sub, N_GROUPS), axis=0, keepdims=True)
        hit = sub == gi
        gsel = jnp.where(hit, 1.0, gsel)
        gs = jnp.where(hit, ninf, gs)
    masked = jnp.concatenate(
        [jnp.where(gsel[g:g + 1, :] > 0.0, sel[g * GROUP_SIZE:(g + 1) * GROUP_SIZE, :], ninf)
         for g in range(N_GROUPS)], axis=0)
    row = lax.broadcasted_iota(I32, (N_EXPERTS, tm), 0)
    ids, ws = [], []
    for _ in range(TOP_K):
        m = jnp.max(masked, axis=0, keepdims=True)
        ii = jnp.min(jnp.where(masked == m, row, N_EXPERTS), axis=0, keepdims=True)
        hit = row == ii
        ws.append(jnp.sum(jnp.where(hit, scores, 0.0), axis=0, keepdims=True))
        ids.append(ii)
        masked = jnp.where(hit, ninf, masked)
    w = jnp.concatenate(ws, axis=0)
    w = w / jnp.sum(w, axis=0, keepdims=True) * ROUTED_SCALE
    return jnp.concatenate(ids, axis=0), w


def _post_mixer_kernel(*refs, n_parts, split_tile):
    ys = refs[:2 * n_parts]
    ws = refs[2 * n_parts:3 * n_parts]
    (xa_ref, xb_ref, gt_ref, g_ref, sc_ref, sh_ref, gtf_ref, rw_ref, rb_ref, s13_ref, s2_ref,
     xs_ref, h_ref, idx_ref, wts_ref, cnt_ref) = refs[3 * n_parts:]
    acc = None
    for p in range(n_parts):
        y = _pair_read(ys[2 * p], ys[2 * p + 1], split_tile)
        d = _dot(y.astype(BF16), ws[p][...])
        acc = d if acc is None else acc + d
    x = _pair_read(xa_ref, xb_ref, split_tile) + gt_ref[...] * acc
    h = _norm_mod(x, g_ref[...], sc_ref[...], sh_ref[...])
    h_ref[...] = _pack_rows(h)
    hh, hl = _split(h)
    rh, rl = _split(rw_ref[...])
    logits = _dot_nt(rh, hh) + _dot_nt(rl, hh) + _dot_nt(rh, hl)
    ids, w = _route(jax.nn.sigmoid(logits), rb_ref[...])
    idx_ref[...] = ids
    wts_ref[...] = w
    _, member = _membership(ids, ids.shape[1])
    cnt = jnp.sum(member, axis=1, keepdims=True)
    cnt_ref[...] = jnp.broadcast_to(cnt, cnt_ref.shape).astype(I32)
    u = _dot(hh, s13_ref[...])
    hs = _silu(u[:, :D_SHARED]) * u[:, D_SHARED:]
    xs_ref[...] = x + gtf_ref[...] * _dot(hs.astype(BF16), s2_ref[...])


def _post_mixer(parts, weights, x_pair, gt, g, sc, sh, gtf, rw_t, rb, s13, s2, bounds, split_tile, name):
    d = x_pair[0].shape[1]
    nt = bounds[-1]
    t = nt * TM
    seq = lambda i: (_seq_of_tile(i, bounds), 0, 0)
    in_specs, args = [], []
    for pair in parts:
        in_specs += _pair_specs(pair, split_tile)
        args += list(pair)
    for w in weights:
        in_specs.append(pl.BlockSpec(w.shape, lambda i: (0, 0)))
        args.append(w)
    modspec = pl.BlockSpec((None, 1, d), seq)
    full = lambda a: pl.BlockSpec(a.shape, lambda i: (0, 0))
    in_specs += _pair_specs(x_pair, split_tile) + [modspec, full(g), modspec, modspec, modspec,
                                                   full(rw_t), full(rb), full(s13), full(s2)]
    args += [*x_pair, gt, g, sc, sh, gtf, rw_t, rb, s13, s2]
    return pl.pallas_call(
        functools.partial(_post_mixer_kernel, n_parts=len(parts), split_tile=split_tile),
        grid=(nt,),
        in_specs=in_specs,
        out_specs=[pl.BlockSpec((TM, d), lambda i: (i, 0)),
                   pl.BlockSpec((TM, d // 2), lambda i: (i, 0)),
                   pl.BlockSpec((TOP_K, TM), lambda i: (0, i)),
                   pl.BlockSpec((TOP_K, TM), lambda i: (0, i)),
                   pl.BlockSpec((None, N_EXPERTS, LANES), lambda i: (i, 0, 0))],
        out_shape=[jax.ShapeDtypeStruct((t, d), F32), jax.ShapeDtypeStruct((t, d // 2), U32),
                   jax.ShapeDtypeStruct((TOP_K, t), I32), jax.ShapeDtypeStruct((TOP_K, t), F32),
                   jax.ShapeDtypeStruct((nt, N_EXPERTS, LANES), I32)],
        compiler_params=_cparams("parallel"), name=name,
    )(*args)


def _membership(ids, n_tok):
    row = lax.broadcasted_iota(I32, (N_EXPERTS, n_tok), 0)
    m = jnp.zeros((N_EXPERTS, n_tok), F32)
    for k in range(TOP_K):
        m = m + (row == ids[k:k + 1, :]).astype(F32)
    return row, m


def _dispatch_tables(cnt_tile):
    cnt = cnt_tile[:, :, 0]
    nt = cnt.shape[0]
    carry = jnp.cumsum(cnt, axis=0) - cnt
    counts = jnp.sum(cnt, axis=0)
    padded = (counts + EXPERT_CHUNK - 1) // EXPERT_CHUNK * EXPERT_CHUNK
    pends = jnp.cumsum(padded)
    pstarts = pends - padded
    base = (pstarts[None, :] + carry).astype(F32).reshape(nt, N_EXPERTS, 1)
    n_slots = nt * TM * TOP_K + N_EXPERTS * EXPERT_CHUNK
    n_chunks = n_slots // EXPERT_CHUNK
    chunk_row = jnp.arange(n_chunks, dtype=I32) * EXPERT_CHUNK
    chunk_e = jnp.minimum(jnp.sum((pends[None, :] <= chunk_row[:, None]).astype(I32), axis=1), N_EXPERTS - 1)
    n_used = (pends[-1] // EXPERT_CHUNK).astype(I32).reshape(1)
    pad_lo = (pstarts + counts).astype(I32)
    return base, chunk_e, n_used, pad_lo, pends.astype(I32), n_chunks


def _rank_kernel(ids_ref, base_ref, tri_ref, slot_ref):
    ids = ids_ref[...]
    tm = ids.shape[1]
    row, m = _membership(ids, tm)
    prefix = _dot(m.astype(BF16), tri_ref[...])
    pos = base_ref[...] + (prefix - m)
    slots = [jnp.sum(jnp.where(row == ids[k:k + 1, :], pos, 0.0), axis=0, keepdims=True) for k in range(TOP_K)]
    slot_ref[...] = jnp.concatenate(slots, axis=0).astype(I32)


def _rank(ids, base):
    k, t = ids.shape
    tri = jnp.asarray(np.triu(np.ones((TM, TM), np.float32)).astype(ml_dtypes.bfloat16))
    return pl.pallas_call(
        _rank_kernel, grid=(t // TM,),
        in_specs=[pl.BlockSpec((k, TM), lambda i: (0, i)),
                  pl.BlockSpec((None, N_EXPERTS, 1), lambda i: (i, 0, 0)),
                  pl.BlockSpec((TM, TM), lambda i: (0, 0))],
        out_specs=pl.BlockSpec((k, TM), lambda i: (0, i)),
        out_shape=jax.ShapeDtypeStruct((k, t), I32),
        compiler_params=_cparams("parallel"), name="moe_rank",
    )(ids, base, tri)


PUSH_UNROLL = 4


def _push_kernel(lo_ref, hi_ref, nu_ref, slot_ref, h_ref, xs_hbm, zero, sem):
    i = pl.program_id(0)
    tm = h_ref.shape[0]
    n_chunks = xs_hbm.shape[0] // EXPERT_CHUNK

    def body(tb, carry):
        for j in range(PUSH_UNROLL):
            t = tb * PUSH_UNROLL + j
            for k in range(TOP_K):
                pltpu.make_async_copy(h_ref.at[pl.ds(t, 1), :], xs_hbm.at[pl.ds(slot_ref[k, t], 1), :], sem).start()
        return carry

    lax.fori_loop(0, tm // PUSH_UNROLL, body, 0)

    @pl.when(i == 0)
    def _():
        zero[...] = jnp.zeros_like(zero)

        def zstart(r, carry):
            pltpu.make_async_copy(zero.at[pl.ds(0, 1), :], xs_hbm.at[pl.ds(r, 1), :], sem).start()
            return carry

        def zwait(r, carry):
            pltpu.make_async_copy(zero.at[pl.ds(0, 1), :], xs_hbm.at[pl.ds(0, 1), :], sem).wait()
            return carry

        def per_expert(e, carry):
            lax.fori_loop(lo_ref[e], hi_ref[e], zstart, 0)
            lax.fori_loop(lo_ref[e], hi_ref[e], zwait, 0)
            return carry

        lax.fori_loop(0, N_EXPERTS, per_expert, 0)

        def chunk_copy(c):
            row0 = pl.multiple_of(c * EXPERT_CHUNK, EXPERT_CHUNK)
            return pltpu.make_async_copy(zero, xs_hbm.at[pl.ds(row0, EXPERT_CHUNK), :], sem)

        def cstart(c, carry):
            chunk_copy(c).start()
            return carry

        def cwait(c, carry):
            chunk_copy(c).wait()
            return carry

        lax.fori_loop(nu_ref[0], n_chunks, cstart, 0)
        lax.fori_loop(nu_ref[0], n_chunks, cwait, 0)

    for _ in range(TOP_K):
        pltpu.make_async_copy(h_ref, xs_hbm.at[pl.ds(0, tm), :], sem).wait()


def _push(slot_of, h, pad_lo, pad_hi, n_used, n_slots):
    t, d = h.shape
    nt = t // TM
    grid_spec = pltpu.PrefetchScalarGridSpec(
        num_scalar_prefetch=3,
        grid=(nt,),
        in_specs=[pl.BlockSpec((TOP_K, TM), lambda i, lo, hi, nu: (0, i), memory_space=pltpu.SMEM),
                  pl.BlockSpec((TM, d), lambda i, lo, hi, nu: (i, 0))],
        out_specs=pl.BlockSpec(memory_space=pl.ANY),
        scratch_shapes=[pltpu.VMEM((EXPERT_CHUNK, d), h.dtype), pltpu.SemaphoreType.DMA(())],
    )
    return pl.pallas_call(
        _push_kernel, grid_spec=grid_spec,
        out_shape=jax.ShapeDtypeStruct((n_slots, d), h.dtype),
        compiler_params=_cparams("arbitrary"), name="moe_push",
    )(pad_lo, pad_hi, n_used, slot_of, h)


def _expert_kernel(ce_ref, nu_ref, x_ref, w1_ref, w3_ref, w2_ref, o_ref, w1b, w3b, w2b):
    c = pl.program_id(0)
    live = c < nu_ref[0]

    @pl.when(jnp.logical_and(live, jnp.logical_or(c == 0, ce_ref[c] != ce_ref[jnp.maximum(c - 1, 0)])))
    def _():
        w1b[...] = w1_ref[...].astype(BF16)
        w3b[...] = w3_ref[...].astype(BF16)
        w2b[...] = w2_ref[...].astype(BF16)

    @pl.when(live)
    def _():
        xa, xb = _unpack_rows(x_ref[...])
        xa = xa.astype(BF16)
        xb = xb.astype(BF16)
        half = xa.shape[1]
        a = _dot(xa, w1b[:half, :]) + _dot(xb, w1b[half:, :])
        b = _dot(xa, w3b[:half, :]) + _dot(xb, w3b[half:, :])
        o_ref[...] = _pack_rows(_dot((_silu(a) * b).astype(BF16), w2b[...]))

    @pl.when(c >= nu_ref[0])
    def _():
        o_ref[...] = jnp.zeros_like(o_ref)


def _expert_ffn(xs, chunk_e, n_used, w1, w3, w2, layer):
    n_slots, dp = xs.shape
    d = 2 * dp
    n_chunks = n_slots // EXPERT_CHUNK
    live = lambda c, nu: jnp.minimum(c, nu[0] - 1)
    wmap = lambda c, ce, nu: (layer, ce[live(c, nu)], 0, 0)
    grid_spec = pltpu.PrefetchScalarGridSpec(
        num_scalar_prefetch=2,
        grid=(n_chunks,),
        in_specs=[
            pl.BlockSpec((EXPERT_CHUNK, dp), lambda c, ce, nu: (live(c, nu), 0)),
            pl.BlockSpec((None, None, d, D_EXPERT), wmap),
            pl.BlockSpec((None, None, d, D_EXPERT), wmap),
            pl.BlockSpec((None, None, D_EXPERT, d), wmap),
        ],
        out_specs=pl.BlockSpec((EXPERT_CHUNK, dp), lambda c, ce, nu: (c, 0)),
        scratch_shapes=[pltpu.VMEM((d, D_EXPERT), BF16), pltpu.VMEM((d, D_EXPERT), BF16),
                        pltpu.VMEM((D_EXPERT, d), BF16)],
    )
    return pl.pallas_call(
        _expert_kernel, grid_spec=grid_spec,
        out_shape=jax.ShapeDtypeStruct((n_slots, dp), U32),
        compiler_params=_cparams("arbitrary"), name="moe_experts",
    )(chunk_e, n_used, xs, w1, w3, w2)


def _combine_kernel(sc_ref, sn_ref, w_ref, xs_ref, gt_ref, ys_hbm, o_ref, buf, sem):
    i = pl.program_id(0)
    n = pl.num_programs(0)
    slot = i % 2

    def issue(s_ref, s):
        def body(rb, carry):
            for j in range(SUBLANES):
                r = rb * SUBLANES + j
                for k in range(TOP_K):
                    pltpu.make_async_copy(ys_hbm.at[pl.ds(s_ref[k, r], 1), :],
                                          buf.at[s, k, pl.ds(r, 1), :], sem.at[s]).start()
            return carry
        lax.fori_loop(0, TC // SUBLANES, body, 0)

    @pl.when(i == 0)
    def _():
        issue(sc_ref, 0)

    @pl.when(i + 1 < n)
    def _():
        issue(sn_ref, 1 - slot)

    for k in range(TOP_K):
        pltpu.make_async_copy(ys_hbm.at[pl.ds(0, TC), :], buf.at[slot, k], sem.at[slot]).wait()
    dp = buf.shape[-1]
    acc_a = acc_b = None
    for k in range(TOP_K):
        wcol = jnp.transpose(jnp.broadcast_to(w_ref[k:k + 1, :], (LANES, TC)))
        wk = _lane_tile(wcol, dp)
        ya, yb = _unpack_rows(buf[slot, k])
        acc_a = ya * wk if acc_a is None else acc_a + ya * wk
        acc_b = yb * wk if acc_b is None else acc_b + yb * wk
    o_ref[...] = xs_ref[...] + gt_ref[...] * jnp.concatenate([acc_a, acc_b], axis=-1)


def _combine(slot_of, wts, xs, gtf, ys, bounds_tc):
    t, d = xs.shape
    nt = t // TC
    seq = lambda i: (_seq_of_tile(i, bounds_tc), 0, 0)
    return pl.pallas_call(
        _combine_kernel, grid=(nt,),
        in_specs=[pl.BlockSpec((TOP_K, TC), lambda i: (0, i), memory_space=pltpu.SMEM),
                  pl.BlockSpec((TOP_K, TC), lambda i: (0, jnp.minimum(i + 1, nt - 1)), memory_space=pltpu.SMEM),
                  pl.BlockSpec((TOP_K, TC), lambda i: (0, i)),
                  pl.BlockSpec((TC, d), lambda i: (i, 0)),
                  pl.BlockSpec((None, 1, d), seq),
                  pl.BlockSpec(memory_space=pl.ANY)],
        out_specs=pl.BlockSpec((TC, d), lambda i: (i, 0)),
        out_shape=jax.ShapeDtypeStruct((t, d), F32),
        scratch_shapes=[pltpu.VMEM((2, TOP_K, TC, ys.shape[1]), ys.dtype), pltpu.SemaphoreType.DMA((2,))],
        compiler_params=_cparams("arbitrary"), name="moe_combine",
    )(slot_of, slot_of, wts, xs, gtf, ys)


def _final_norm_kernel(x_ref, g_ref, o_ref):
    x = x_ref[...]
    ms = jnp.mean(x * x, axis=-1, keepdims=True)
    o_ref[...] = x * lax.rsqrt(ms + EPS) * g_ref[...]


def _final_norm(x, g, tile_off, n_tiles):
    d = x.shape[1]
    return pl.pallas_call(
        _final_norm_kernel, grid=(n_tiles,),
        in_specs=[pl.BlockSpec((TM, d), lambda i: (i + tile_off, 0)), pl.BlockSpec((1, d), lambda i: (0, 0))],
        out_specs=pl.BlockSpec((TM, d), lambda i: (i, 0)),
        out_shape=jax.ShapeDtypeStruct((n_tiles * TM, d), F32),
        compiler_params=_cparams("parallel"), name="final_norm",
    )(x, g)


def kernel(x_prompt, x_sample, c_prompt, c_sample, ada_w, ada_b, g_mix, g_ffn, g_final,
           w_in_ab, w_out_ab, conv_w, conv_b, filt_w1, filt_b1, filt_f1, filt_w2, filt_b2, filt_f2,
           filt_w3, decay, hyena_bias, w_qkv, w_out_c, rpb,
           router_w, router_b, exp_w1, exp_w3, exp_w2, sh_w1, sh_w3, sh_w2):
    d = D_MODEL
    bp, lp, _ = x_prompt.shape
    bs, ls, _ = x_sample.shape
    trunks = ((0, bp, lp), (bp * lp, bs, ls))
    t_all = bp * lp + bs * ls
    n_seq = bp + bs
    depth = ada_w.shape[0]
    assert lp % TM == 0 and ls % TM == 0
    bounds, acc = [], 0
    for _, b, l in trunks:
        for _ in range(b):
            acc += l // TM
            bounds.append(acc)
    bounds = tuple(bounds)
    bounds_tc = tuple(b * (TM // TC) for b in bounds)
    split_tile = bp * lp // TM

    x_pair = (x_prompt.reshape(bp * lp, d), x_sample.reshape(bs * ls, d))
    c_all = jnp.concatenate([c_prompt, c_sample, jnp.zeros((SUBLANES - n_seq, d), F32)], axis=0)
    mod = _modulation(c_all, ada_w, ada_b)[:, :n_seq]
    mod = mod.reshape(depth, n_seq, 6, 1, d)

    bands = jnp.linspace(1e-4, POS_BANDS - 1, POS_BANDS, dtype=F32).reshape(1, POS_BANDS)
    bias_tab = None

    for l in range(depth):
        sh_m, sc_m, gt_m, sh_f, sc_f, gt_f = (mod[l, :, j] for j in range(6))
        g_m = g_mix[l].reshape(1, d)
        i = l // 2
        if l % 2 == 0:
            u_f, u_h = _norm_mod_matmul(x_pair, g_m, sc_m, sh_m, w_in_ab[i].astype(BF16),
                                        (D_FNET, 3 * D_HYENA), (F32, F32), bounds, split_tile, "in_proj_ab")
            z, x0 = _hyena_pre(u_h, conv_w[i], conv_b[i], bounds)
            ya, yb = [], []
            for tok_off, b, sl in trunks:
                plan = _HyenaPlan(sl)
                hc, ss = _hyena_filter(sl, bands, filt_w1[i], filt_b1[i], filt_f1[i], filt_w2[i],
                                       filt_b2[i], filt_f2[i], filt_w3[i], decay[i])
                hr, hi = _hyena_spectrum(plan, hc)
                yb.append(_hyena_longconv(plan, z, x0, hyena_bias[i], hr, hi, ss, tok_off, b, sl))
                ya.append(_fnet_mix(u_f, tok_off, b, sl))
            parts = [tuple(ya), tuple(yb)]
            w_out = w_out_ab[i].astype(BF16)
            weights = [w_out[:D_FNET], w_out[D_FNET:]]
        else:
            q, k, v = _norm_mod_matmul(x_pair, g_m, sc_m, sh_m, w_qkv[i].astype(BF16),
                                       (d, d, d), (BF16, BF16, BF16), bounds, split_tile, "qkv_proj")
            bias_tab = _attn_bias_table(rpb[i])
            o = tuple(_attention(q, k, v, bias_tab, tok_off, b, sl) for tok_off, b, sl in trunks)
            parts = [o]
            weights = [w_out_c[i].astype(BF16)]
        s13 = jnp.concatenate([sh_w1[l], sh_w3[l]], axis=1).astype(BF16)
        xs, h, idx, wts, cnt_tile = _post_mixer(
            parts, weights, x_pair, gt_m, g_ffn[l].reshape(1, d), sc_f, sh_f, gt_f,
            router_w[l].T, router_b[l].reshape(N_EXPERTS, 1), s13, sh_w2[l].astype(BF16),
            bounds, split_tile, "post_mixer_%d" % l)
        base, chunk_e, n_used, pad_lo, pad_hi, n_chunks = _dispatch_tables(cnt_tile)
        slot_of = _rank(idx, base)
        xg = _push(slot_of, h, pad_lo, pad_hi, n_used, n_chunks * EXPERT_CHUNK)
        ys = _expert_ffn(xg, chunk_e, n_used, exp_w1, exp_w3, exp_w2, l)
        x = _combine(slot_of, wts, xs, gt_f, ys, bounds_tc)
        x_pair = (x, x)

    g_fin = g_final.reshape(1, d)
    y_prompt = _final_norm(x, g_fin, 0, bp * lp // TM).reshape(bp, lp, d)
    y_sample = _final_norm(x, g_fin, bp * lp // TM, bs * ls // TM).reshape(bs, ls, d)
    return (y_prompt, y_sample)
```

```python
import functools
import math

import ml_dtypes
import numpy as np

import jax
import jax.numpy as jnp
from jax import lax
from jax.experimental import pallas as pl
from jax.experimental.pallas import tpu as pltpu

F32 = jnp.float32
BF16 = jnp.bfloat16
I32 = jnp.int32
U32 = jnp.uint32

D_MODEL = 1024
GRID_W = 64
D_FNET = 512
N_FNET_GROUPS = 4
D_GROUP = D_FNET // N_FNET_GROUPS
D_HYENA = 512
POS_BANDS = 16
N_HEADS = 16
HEAD_DIM = 64
WIN_ROWS = 8
WIN_COLS = 16
N_EXPERTS = 64
TOP_K = 8
N_GROUPS = 8
GROUP_SIZE = N_EXPERTS // N_GROUPS
TOPK_GROUPS = 4
D_EXPERT = 256
D_SHARED = 256
ROUTED_SCALE = 2.5
EXPERT_CHUNK = 256
EPS = 1e-6
NEG_BIG = -1e30

LANES = 128
SUBLANES = 8
TM = 512
TC = 128
NB = 8
VMEM_LIMIT = 56 * 1024 * 1024


def _cparams(*sem):
    return pltpu.CompilerParams(dimension_semantics=sem, vmem_limit_bytes=VMEM_LIMIT)


def _dot(a, b):
    return jnp.dot(a, b, preferred_element_type=F32)


def _dot_nt(a, b):
    return lax.dot_general(a, b, (((1,), (1,)), ((), ())), preferred_element_type=F32)


def _split(x):
    hi = x.astype(BF16)
    lo = (x - hi.astype(F32)).astype(BF16)
    return hi, lo


def _dot3_lc(mh, ml, x):
    xh, xl = _split(x)
    return _dot(mh, xh) + _dot(mh, xl) + _dot(ml, xh)


def _dot3_rc(x, mh, ml):
    xh, xl = _split(x)
    return _dot(xh, mh) + _dot(xl, mh) + _dot(xh, ml)


def _dot3(a, b):
    ah, al = _split(a)
    bh, bl = _split(b)
    return _dot(ah, bh) + _dot(al, bh) + _dot(ah, bl)


def _silu(x):
    return x * jax.nn.sigmoid(x)


def _lane_tile(t, width):
    return jnp.concatenate([t] * (width // LANES), axis=-1)


def _pack_rows(x):
    n = x.shape[1] // 2
    xr = x.astype(BF16).astype(F32)
    hi = lax.bitcast_convert_type(xr[:, :n], U32)
    lo = lax.bitcast_convert_type(xr[:, n:], U32)
    return hi | (lo >> 16)


def _unpack_rows(p):
    hi = lax.bitcast_convert_type(p & jnp.uint32(0xFFFF0000), F32)
    lo = lax.bitcast_convert_type(p << 16, F32)
    return hi, lo


def _cmul(ar, ai, br, bi):
    return ar * br - ai * bi, ar * bi + ai * br


def _norm_mod(x, g, sc, sh):
    ms = jnp.mean(x * x, axis=-1, keepdims=True)
    return (x * lax.rsqrt(ms + EPS) * g) * (1.0 + sc) + sh


def _seq_of_tile(i, bounds):
    s = jnp.int32(0)
    for b in bounds[:-1]:
        s = s + (i >= b).astype(I32)
    return s


def _hilo(m):
    m = np.asarray(m, np.float64)
    hi = m.astype(np.float32).astype(ml_dtypes.bfloat16)
    lo = (m - hi.astype(np.float64)).astype(np.float32).astype(ml_dtypes.bfloat16)
    return jnp.asarray(hi), jnp.asarray(lo)


def _cos_sin(n):
    j = np.arange(n)
    ang = 2.0 * np.pi * ((j[:, None] * j[None, :]) % n) / n
    return np.cos(ang), np.sin(ang)


def _twiddle(n_slab, n_row, n_total):
    s = jnp.arange(n_slab, dtype=I32)[:, None]
    r = jnp.arange(n_row, dtype=I32)[None, :]
    ang = ((s * r) % n_total).astype(F32) * (2.0 * math.pi / n_total)
    shape = (n_slab, n_row, LANES)
    return (jnp.broadcast_to(jnp.cos(ang)[:, :, None], shape),
            jnp.broadcast_to(-jnp.sin(ang)[:, :, None], shape))


def _mod_kernel(c_ref, w_ref, b_ref, o_ref):
    o_ref[...] = _dot3(_silu(c_ref[...]), w_ref[...]) + b_ref[...]


def _modulation(c_pad, ada_w, ada_b):
    depth, d, n = ada_w.shape
    tn = 1536
    return pl.pallas_call(
        _mod_kernel,
        grid=(depth, n // tn),
        in_specs=[pl.BlockSpec((SUBLANES, d), lambda l, j: (0, 0)),
                  pl.BlockSpec((None, d, tn), lambda l, j: (l, 0, j)),
                  pl.BlockSpec((None, 1, tn), lambda l, j: (l, 0, j))],
        out_specs=pl.BlockSpec((None, SUBLANES, tn), lambda l, j: (l, 0, j)),
        out_shape=jax.ShapeDtypeStruct((depth, SUBLANES, n), F32),
        compiler_params=_cparams("parallel", "parallel"),
        name="adaln_mod",
    )(c_pad, ada_w, ada_b.reshape(depth, 1, n))


def _pair_specs(pair, split_tile):
    a, b = pair
    w = a.shape[1]
    b_off = 0 if a is b else split_tile
    return [pl.BlockSpec((TM, w), lambda i: (jnp.minimum(i, split_tile - 1), 0)),
            pl.BlockSpec((TM, w), lambda i: (jnp.maximum(i, split_tile) - b_off, 0))]


def _pair_read(a_ref, b_ref, split_tile):
    return jnp.where(pl.program_id(0) < split_tile, a_ref[...], b_ref[...])


def _nmm_kernel(xa_ref, xb_ref, g_ref, sc_ref, sh_ref, w_ref, *o_refs, split_tile):
    x = _pair_read(xa_ref, xb_ref, split_tile)
    h = _norm_mod(x, g_ref[...], sc_ref[...], sh_ref[...])
    u = _dot(h.astype(BF16), w_ref[...])
    off = 0
    for o in o_refs:
        n = o.shape[-1]
        o[...] = u[:, off:off + n].astype(o.dtype)
        off += n


def _norm_mod_matmul(x_pair, g, sc, sh, w_bf16, splits, out_dtypes, bounds, split_tile, name):
    d, n = w_bf16.shape
    nt = bounds[-1]
    seq = lambda i: (_seq_of_tile(i, bounds), 0, 0)
    return pl.pallas_call(
        functools.partial(_nmm_kernel, split_tile=split_tile),
        grid=(nt,),
        in_specs=_pair_specs(x_pair, split_tile) + [
            pl.BlockSpec((1, d), lambda i: (0, 0)),
            pl.BlockSpec((None, 1, d), seq),
            pl.BlockSpec((None, 1, d), seq),
            pl.BlockSpec((d, n), lambda i: (0, 0))],
        out_specs=[pl.BlockSpec((TM, s), lambda i: (i, 0)) for s in splits],
        out_shape=[jax.ShapeDtypeStruct((nt * TM, s), dt) for s, dt in zip(splits, out_dtypes)],
        compiler_params=_cparams("parallel"),
        name=name,
    )(*x_pair, g, sc, sh, w_bf16)


def _hyena_pre_kernel(prev_ref, cur_ref, next_ref, cw_ref, cb_ref, z_ref, x0_ref, *, first_tiles, last_tiles):
    i = pl.program_id(0)
    cur = cur_ref[...]
    tm = cur.shape[0]
    is_first = functools.reduce(jnp.logical_or, [i == f for f in first_tiles])
    is_last = functools.reduce(jnp.logical_or, [i == f for f in last_tiles])
    prow = jnp.where(is_first, 0.0, prev_ref[SUBLANES - 1:SUBLANES, :])
    nrow = jnp.where(is_last, 0.0, next_ref[0:1, :])
    rid = lax.broadcasted_iota(I32, (tm, 1), 0)
    up = jnp.where(rid == 0, prow, pltpu.roll(cur, 1, 0))
    dn = jnp.where(rid == tm - 1, nrow, pltpu.roll(cur, tm - 1, 0))
    cw = cw_ref[...]
    uc = cb_ref[...] + (up * cw[0:1, :] + cur * cw[1:2, :] + dn * cw[2:3, :])
    x0_ref[...] = uc[:, :D_HYENA]
    z_ref[...] = uc[:, 2 * D_HYENA:] * uc[:, D_HYENA:2 * D_HYENA]


def _hyena_pre(u_h, conv_w, conv_b, bounds):
    t, c = u_h.shape
    nt = t // TM
    rb = TM // SUBLANES
    first_tiles = (0,) + tuple(bounds[:-1])
    last_tiles = tuple(b - 1 for b in bounds)
    kern = functools.partial(_hyena_pre_kernel, first_tiles=first_tiles, last_tiles=last_tiles)
    return pl.pallas_call(
        kern,
        grid=(nt,),
        in_specs=[pl.BlockSpec((SUBLANES, c), lambda i: (jnp.maximum(i * rb - 1, 0), 0)),
                  pl.BlockSpec((TM, c), lambda i: (i, 0)),
                  pl.BlockSpec((SUBLANES, c), lambda i: (jnp.minimum((i + 1) * rb, nt * rb - 1), 0)),
                  pl.BlockSpec((3, c), lambda i: (0, 0)),
                  pl.BlockSpec((1, c), lambda i: (0, 0))],
        out_specs=[pl.BlockSpec((TM, D_HYENA), lambda i: (i, 0)),
                   pl.BlockSpec((TM, D_HYENA), lambda i: (i, 0))],
        out_shape=[jax.ShapeDtypeStruct((t, D_HYENA), F32), jax.ShapeDtypeStruct((t, D_HYENA), F32)],
        compiler_params=_cparams("parallel"),
        name="hyena_pre",
    )(u_h, u_h, u_h, conv_w, conv_b.reshape(1, c))


def _filter_kernel(bands_ref, w1_ref, b1_ref, f1_ref, w2_ref, b2_ref, f2_ref, w3_ref, dec_ref,
                   hc_ref, ss_ref, *, seq_len):
    i = pl.program_id(0)
    tm = hc_ref.shape[0]
    is_bwd = i >= seq_len // tm

    def offset_of(row):
        return jnp.where(is_bwd, 2 * seq_len - row, row).astype(F32)

    n = offset_of(i * tm + lax.broadcasted_iota(I32, (1, tm), 1))
    t = n / float(seq_len - 1)
    ang = (2.0 * math.pi / seq_len) * bands_ref[...] * n
    pre = (w1_ref[:, 0:1] * t + _dot3(w1_ref[:, 1:1 + POS_BANDS], jnp.cos(ang))
           + _dot3(w1_ref[:, 1 + POS_BANDS:], -jnp.sin(ang)) + b1_ref[...])
    h = jnp.sin(f1_ref[...] * pre)
    h = jnp.sin(f2_ref[...] * (_dot3(w2_ref[...], h) + b2_ref[...]))
    w3 = w3_ref[...]
    w3 = jnp.where(is_bwd, w3[:, D_HYENA:], w3[:, :D_HYENA])
    hh, hl = _split(h)
    wh, wl = _split(w3)
    tn = lambda a, b: lax.dot_general(a, b, (((0,), (0,)), ((), ())), preferred_element_type=F32)
    h3 = tn(hh, wh) + tn(hl, wh) + tn(hh, wl)
    row = i * tm + lax.broadcasted_iota(I32, (tm, 1), 0)
    t_col = offset_of(row) / float(seq_len - 1)
    dec = jnp.abs(dec_ref[...])
    hc = h3 * jnp.exp(-t_col * jnp.where(is_bwd, dec[1:2, :], dec[0:1, :]))
    hc = jnp.where(row == seq_len, 0.0, hc)
    hc_ref[...] = hc

    @pl.when(i == 0)
    def _():
        ss_ref[...] = jnp.zeros_like(ss_ref)

    ss_ref[...] += jnp.sum(hc * hc, axis=0, keepdims=True)


def _hyena_filter(seq_len, bands, w1, b1, f1, w2, b2, f2, w3, decay):
    tm = 512
    full = lambda a: pl.BlockSpec(a.shape, lambda i: (0,) * a.ndim)
    col = lambda a: a.reshape(-1, 1)
    args = (col(bands), w1.T, col(b1), col(f1), w2.T, col(b2), col(f2), w3, decay)
    return pl.pallas_call(
        functools.partial(_filter_kernel, seq_len=seq_len),
        grid=(2 * seq_len // tm,),
        in_specs=[full(a) for a in args],
        out_specs=[pl.BlockSpec((tm, D_HYENA), lambda i: (i, 0)),
                   pl.BlockSpec((1, D_HYENA), lambda i: (0, 0))],
        out_shape=[jax.ShapeDtypeStruct((2 * seq_len, D_HYENA), F32),
                   jax.ShapeDtypeStruct((1, D_HYENA), F32)],
        compiler_params=_cparams("arbitrary"),
        name="hyena_filter",
    )(*args)


def _strided_fwd_kernel(x_ref, mh_ref, ml_ref, or_ref, oi_ref):
    n1 = or_ref.shape[0]
    mh = mh_ref[...]
    ml = ml_ref[...]
    for j in range(NB):
        a = _dot3_lc(mh, ml, x_ref[:, j, :])
        or_ref[:, j, :] = a[:n1]
        oi_ref[:, j, :] = a[n1:]


def _strided_fwd(x3, row_block, rows_in, m, batch, name):
    _, n2, ch = x3.shape
    mh, ml = m
    n1 = mh.shape[0] // 2
    out = jax.ShapeDtypeStruct((batch, n1, n2, ch), F32)
    ospec = pl.BlockSpec((None, n1, NB, ch), lambda b, j: (b, 0, j, 0))
    return pl.pallas_call(
        _strided_fwd_kernel,
        grid=(batch, n2 // NB),
        in_specs=[pl.BlockSpec((rows_in, NB, ch), lambda b, j: (row_block + b, j, 0)),
                  pl.BlockSpec(mh.shape, lambda b, j: (0, 0)),
                  pl.BlockSpec(ml.shape, lambda b, j: (0, 0))],
        out_specs=[ospec, ospec],
        out_shape=[out, out],
        compiler_params=_cparams("parallel", "parallel"),
        name=name,
    )(x3, mh, ml)


def _slab_specs(n1, n2, ch, batched):
    if batched:
        return pl.BlockSpec((None, NB, n2, ch), lambda b, k: (b, k, 0, 0))
    return pl.BlockSpec((None, NB, n2, ch), lambda b, k: (0, k, 0, 0))


def _slab_fwd_kernel(ar_ref, ai_ref, twr_ref, twi_ref, fh_ref, fl_ref, or_ref, oi_ref):
    n2, ch = ar_ref.shape[1:]
    fh = fh_ref[...]
    fl = fl_ref[...]
    for s in range(NB):
        twr = _lane_tile(twr_ref[s], ch)
        twi = _lane_tile(twi_ref[s], ch)
        br, bi = _cmul(ar_ref[s], ai_ref[s], twr, twi)
        x = _dot3_lc(fh, fl, jnp.concatenate([br, bi], axis=0))
        or_ref[s] = x[:n2]
        oi_ref[s] = x[n2:]


def _slab_conv_kernel(ar_ref, ai_ref, hr_ref, hi_ref, ss_ref, twr_ref, twi_ref,
                      fh_ref, fl_ref, gh_ref, gl_ref, or_ref, oi_ref):
    n2, ch = ar_ref.shape[1:]
    fh = fh_ref[...]
    fl = fl_ref[...]
    gh = gh_ref[...]
    gl = gl_ref[...]
    scale = lax.rsqrt(ss_ref[...] + EPS)
    for s in range(NB):
        twr = _lane_tile(twr_ref[s], ch)
        twi = _lane_tile(twi_ref[s], ch)
        br, bi = _cmul(ar_ref[s], ai_ref[s], twr, twi)
        x = _dot3_lc(fh, fl, jnp.concatenate([br, bi], axis=0))
        pr, pi = _cmul(x[:n2], x[n2:], hr_ref[s] * scale, hi_ref[s] * scale)
        g = _dot3_lc(gh, gl, jnp.concatenate([pr, pi], axis=0))
        qr, qi = _cmul(g[:n2], g[n2:], twr, -twi)
        or_ref[s] = qr
        oi_ref[s] = qi


def _hyena_out_kernel(gr_ref, gi_ref, z_ref, x0_ref, hb_ref, mh_ref, ml_ref, o_ref):
    mh = mh_ref[...]
    ml = ml_ref[...]
    hb = hb_ref[...]
    for j in range(NB):
        g = jnp.concatenate([gr_ref[:, j, :], gi_ref[:, j, :]], axis=0)
        zc = _dot3_lc(mh, ml, g)
        o_ref[:, j, :] = x0_ref[:, j, :] * (zc + z_ref[:, j, :] * hb)


class _HyenaPlan:
    def __init__(self, seq_len):
        n = 2 * seq_len
        self.n = n
        self.n1 = 256 if n >= 32768 else 128
        self.n2 = n // self.n1
        self.half = self.n1 // 2
        c1, s1 = _cos_sin(self.n1)
        c2, s2 = _cos_sin(self.n2)
        h = self.half
        self.m1 = _hilo(np.concatenate([c1, -s1], axis=0))
        self.m1_half = _hilo(np.concatenate([c1[:, :h], -s1[:, :h]], axis=0))
        self.f2 = _hilo(np.block([[c2, s2], [-s2, c2]]))
        self.g2 = _hilo(np.block([[c2, -s2], [s2, c2]]) / n)
        self.m2 = _hilo(np.concatenate([c1[:h], -s1[:h]], axis=1))


def _hyena_spectrum(plan, hc):
    n1, n2 = plan.n1, plan.n2
    ch = hc.shape[1]
    ar, ai = _strided_fwd(hc.reshape(n1, n2, ch), 0, n1, plan.m1, 1, "hyena_filt_s1")
    twr, twi = _twiddle(n1, n2, plan.n)
    fh, fl = plan.f2
    spec = _slab_specs(n1, n2, ch, True)
    tspec = pl.BlockSpec((NB, n2, LANES), lambda b, k: (k, 0, 0))
    cspec = pl.BlockSpec(fh.shape, lambda b, k: (0, 0))
    out = jax.ShapeDtypeStruct((1, n1, n2, ch), F32)
    return pl.pallas_call(
        _slab_fwd_kernel, grid=(1, n1 // NB),
        in_specs=[spec, spec, tspec, tspec, cspec, cspec],
        out_specs=[spec, spec], out_shape=[out, out],
        compiler_params=_cparams("parallel", "parallel"), name="hyena_filt_slab",
    )(ar, ai, twr, twi, fh, fl)


def _hyena_longconv(plan, z, x0, hyena_bias, hr, hi, ss, tok_off, batch, seq_len):
    n1, n2, half = plan.n1, plan.n2, plan.half
    t, ch = z.shape
    assert tok_off % seq_len == 0 and seq_len == half * n2
    row_block = tok_off // seq_len
    z3 = z.reshape(t // n2, n2, ch)
    x03 = x0.reshape(t // n2, n2, ch)
    ar, ai = _strided_fwd(z3, row_block, half, plan.m1_half, batch, "hyena_s1")
    twr, twi = _twiddle(n1, n2, plan.n)
    fh, fl = plan.f2
    gh, gl = plan.g2
    spec = _slab_specs(n1, n2, ch, True)
    hspec = _slab_specs(n1, n2, ch, False)
    tspec = pl.BlockSpec((NB, n2, LANES), lambda b, k: (k, 0, 0))
    cspec = pl.BlockSpec(fh.shape, lambda b, k: (0, 0))
    out = jax.ShapeDtypeStruct((batch, n1, n2, ch), F32)
    gr, gi = pl.pallas_call(
        _slab_conv_kernel, grid=(batch, n1 // NB),
        in_specs=[spec, spec, hspec, hspec, pl.BlockSpec((1, ch), lambda b, k: (0, 0)),
                  tspec, tspec, cspec, cspec, cspec, cspec],
        out_specs=[spec, spec], out_shape=[out, out],
        compiler_params=_cparams("parallel", "parallel"), name="hyena_slab_conv",
    )(ar, ai, hr, hi, ss, twr, twi, fh, fl, gh, gl)
    mh, ml = plan.m2
    gspec = pl.BlockSpec((None, n1, NB, ch), lambda b, j: (b, 0, j, 0))
    xspec = pl.BlockSpec((half, NB, ch), lambda b, j: (row_block + b, j, 0))
    y = pl.pallas_call(
        _hyena_out_kernel, grid=(batch, n2 // NB),
        in_specs=[gspec, gspec, xspec, xspec, pl.BlockSpec((1, ch), lambda b, j: (0, 0)),
                  pl.BlockSpec(mh.shape, lambda b, j: (0, 0)), pl.BlockSpec(ml.shape, lambda b, j: (0, 0))],
        out_specs=pl.BlockSpec((half, NB, ch), lambda b, j: (b, j, 0)),
        out_shape=jax.ShapeDtypeStruct((batch * half, n2, ch), F32),
        compiler_params=_cparams("parallel", "parallel"), name="hyena_s2",
    )(gr, gi, z3, x03, hyena_bias.reshape(1, ch), mh, ml)
    return y.reshape(batch * seq_len, ch)


def _fnet_s1_kernel(x_ref, wh_ref, wl_ref, mh_ref, ml_ref, or_ref, oi_ref):
    n1 = or_ref.shape[0]
    wh = wh_ref[...]
    wl = wl_ref[...]
    mh = mh_ref[...]
    ml = ml_ref[...]
    for j in range(NB):
        x = x_ref[:, j, :]
        zr, zi = [], []
        for g in range(N_FNET_GROUPS):
            zg = _dot3_rc(x[:, g * D_GROUP:(g + 1) * D_GROUP], wh, wl)
            zr.append(zg[:, :D_GROUP])
            zi.append(zg[:, D_GROUP:])
        z = jnp.concatenate([jnp.concatenate(zr, axis=1), jnp.concatenate(zi, axis=1)], axis=0)
        a = _dot3_lc(mh, ml, z)
        or_ref[:, j, :] = a[:n1]
        oi_ref[:, j, :] = a[n1:]


def _fnet_slab_kernel(ar_ref, ai_ref, twr_ref, twi_ref, fh_ref, fl_ref, o_ref, *, scale):
    ch = ar_ref.shape[2]
    fh = fh_ref[...]
    fl = fl_ref[...]
    for s in range(NB):
        twr = _lane_tile(twr_ref[s], ch)
        twi = _lane_tile(twi_ref[s], ch)
        br, bi = _cmul(ar_ref[s], ai_ref[s], twr, twi)
        o_ref[:, s, :] = _dot3_lc(fh, fl, jnp.concatenate([br, bi], axis=0)) * scale


def _fnet_mix(u_f, tok_off, batch, seq_len):
    t, ch = u_f.shape
    n2 = 128
    n1 = seq_len // n2
    assert tok_off % seq_len == 0
    row_block = tok_off // seq_len
    c1, s1 = _cos_sin(n1)
    c2, s2 = _cos_sin(n2)
    cg, sg = _cos_sin(D_GROUP)
    wh, wl = _hilo(np.concatenate([cg, -sg], axis=1))
    mh, ml = _hilo(np.block([[c1, s1], [-s1, c1]]))
    fh, fl = _hilo(np.concatenate([c2, s2], axis=1))
    u3 = u_f.reshape(t // n2, n2, ch)
    out = jax.ShapeDtypeStruct((batch, n1, n2, ch), F32)
    ospec = pl.BlockSpec((None, n1, NB, ch), lambda b, j: (b, 0, j, 0))
    c2d = lambda a: pl.BlockSpec(a.shape, lambda b, j: (0, 0))
    ar, ai = pl.pallas_call(
        _fnet_s1_kernel, grid=(batch, n2 // NB),
        in_specs=[pl.BlockSpec((n1, NB, ch), lambda b, j: (row_block + b, j, 0)),
                  c2d(wh), c2d(wl), c2d(mh), c2d(ml)],
        out_specs=[ospec, ospec], out_shape=[out, out],
        compiler_params=_cparams("parallel", "parallel"), name="fnet_s1",
    )(u3, wh, wl, mh, ml)
    twr, twi = _twiddle(n1, n2, seq_len)
    spec = _slab_specs(n1, n2, ch, True)
    tspec = pl.BlockSpec((NB, n2, LANES), lambda b, k: (k, 0, 0))
    scale = 1.0 / math.sqrt(seq_len * D_GROUP)
    y = pl.pallas_call(
        functools.partial(_fnet_slab_kernel, scale=scale), grid=(batch, n1 // NB),
        in_specs=[spec, spec, tspec, tspec, c2d(fh), c2d(fl)],
        out_specs=pl.BlockSpec((None, n2, NB, ch), lambda b, k: (b, 0, k, 0)),
        out_shape=jax.ShapeDtypeStruct((batch, n2, n1, ch), F32),
        compiler_params=_cparams("parallel", "parallel"), name="fnet_slab",
    )(ar, ai, twr, twi, fh, fl)
    return y.reshape(batch * seq_len, ch)


def _attn_kernel(q_ref, k_ref, v_ref, b_ref, o_ref):
    scale = HEAD_DIM ** -0.5
    nq = q_ref.shape[0]
    kw = k_ref.shape[0]
    low = lax.broadcasted_iota(I32, (nq, LANES), 1) < HEAD_DIM
    outs = []
    for p in range(N_HEADS // 2):
        sl = slice(p * LANES, (p + 1) * LANES)
        q2 = q_ref[:, sl]
        zero = jnp.zeros_like(q2)
        qq = jnp.concatenate([jnp.where(low, q2, zero), jnp.where(low, zero, q2)], axis=0)
        s = _dot_nt(qq, k_ref[:, sl]) * scale + b_ref[2 * p:2 * p + 2].reshape(2 * nq, kw)
        m = jnp.max(s, axis=-1, keepdims=True)
        e = jnp.exp(s - m)
        l = jnp.sum(e, axis=-1, keepdims=True)
        o = _dot(e.astype(BF16), v_ref[:, sl]) / l
        outs.append(jnp.where(low, o[:nq], o[nq:]))
    o_ref[...] = jnp.concatenate(outs, axis=-1).astype(o_ref.dtype)


def _attn_bias_table(rpb):
    cols = jnp.arange(GRID_W)
    col_start = jnp.clip(cols - WIN_COLS // 2, 0, GRID_W - WIN_COLS)
    kc = jnp.arange(GRID_W)
    inside = (kc[None, :] >= col_start[:, None]) & (kc[None, :] < col_start[:, None] + WIN_COLS)
    cidx = jnp.clip(kc[None, :] - cols[:, None] + (WIN_COLS - 1), 0, 2 * WIN_COLS - 2)
    dd = jnp.arange(WIN_ROWS)
    ridx = (WIN_ROWS - 1) - dd[:, None] + jnp.arange(WIN_ROWS)[None, :]
    tab = rpb.astype(F32)[:, ridx][:, :, :, cidx]
    tab = jnp.where(inside[None, None, None], tab, NEG_BIG)
    tab = tab.transpose(1, 0, 3, 2, 4)
    return tab.reshape(WIN_ROWS, N_HEADS, GRID_W, WIN_ROWS * GRID_W)


def _attention(q, k, v, bias_tab, tok_off, batch, seq_len):
    rows = seq_len // GRID_W
    assert rows >= WIN_ROWS
    d = q.shape[1]
    kw = WIN_ROWS * GRID_W

    def rs_of(r):
        return jnp.clip(r - WIN_ROWS // 2, 0, rows - WIN_ROWS)

    kspec = pl.BlockSpec((pl.Element(kw), pl.Element(d)),
                         lambda b, r: (pl.multiple_of(tok_off + b * seq_len + rs_of(r) * GRID_W, GRID_W), 0))
    return pl.pallas_call(
        _attn_kernel, grid=(batch, rows),
        in_specs=[pl.BlockSpec((GRID_W, d), lambda b, r: (tok_off // GRID_W + b * rows + r, 0)),
                  kspec, kspec,
                  pl.BlockSpec((None, N_HEADS, GRID_W, kw), lambda b, r: (r - rs_of(r), 0, 0, 0))],
        out_specs=pl.BlockSpec((GRID_W, d), lambda b, r: (b * rows + r, 0)),
        out_shape=jax.ShapeDtypeStruct((batch * seq_len, d), BF16),
        compiler_params=_cparams("parallel", "parallel"), name="natten",
    )(q, k, v, bias_tab)


def _route(scores, rb):
    sel = scores + rb
    tm = sel.shape[1]
    sub = lax.broadcasted_iota(I32, (GROUP_SIZE, tm), 0)
    ninf = -jnp.inf
    groups = []
    for g in range(N_GROUPS):
        sg = sel[g * GROUP_SIZE:(g + 1) * GROUP_SIZE, :]
        m1 = jnp.max(sg, axis=0, keepdims=True)
        i1 = jnp.min(jnp.where(sg == m1, sub, GROUP_SIZE), axis=0, keepdims=True)
        m2 = jnp.max(jnp.where(sub == i1, ninf, sg), axis=0, keepdims=True)
        groups.append(m1 + m2)
    gs = jnp.concatenate(groups, axis=0)
    gsel = jnp.zeros((N_GROUPS, tm), F32)
    for _ in range(TOPK_GROUPS):
        m = jnp.max(gs, axis=0, keepdims=True)
        gi = jnp.min(jnp.where(gs == m, sub, N_GROUPS), axis=0, keepdims=True)
        hit = sub == gi
        gsel = jnp.where(hit, 1.0, gsel)
        gs = jnp.where(hit, ninf, gs)
    masked = jnp.concatenate(
        [jnp.where(gsel[g:g + 1, :] > 0.0, sel[g * GROUP_SIZE:(g + 1) * GROUP_SIZE, :], ninf)
         for g in range(N_GROUPS)], axis=0)
    row = lax.broadcasted_iota(I32, (N_EXPERTS, tm), 0)
    ids, ws = [], []
    for _ in range(TOP_K):
        m = jnp.max(masked, axis=0, keepdims=True)
        ii = jnp.min(jnp.where(masked == m, row, N_EXPERTS), axis=0, keepdims=True)
        hit = row == ii
        ws.append(jnp.sum(jnp.where(hit, scores, 0.0), axis=0, keepdims=True))
        ids.append(ii)
        masked = jnp.where(hit, ninf, masked)
    w = jnp.concatenate(ws, axis=0)
    w = w / jnp.sum(w, axis=0, keepdims=True) * ROUTED_SCALE
    return jnp.concatenate(ids, axis=0), w


def _post_mixer_kernel(*refs, n_parts, split_tile):
    ys = refs[:2 * n_parts]
    ws = refs[2 * n_parts:3 * n_parts]
    (xa_ref, xb_ref, gt_ref, g_ref, sc_ref, sh_ref, gtf_ref, rw_ref, rb_ref, s13_ref, s2_ref,
     xs_ref, h_ref, idx_ref, wts_ref, cnt_ref) = refs[3 * n_parts:]
    acc = None
    for p in range(n_parts):
        y = _pair_read(ys[2 * p], ys[2 * p + 1], split_tile)
        d = _dot(y.astype(BF16), ws[p][...])
        acc = d if acc is None else acc + d
    x = _pair_read(xa_ref, xb_ref, split_tile) + gt_ref[...] * acc
    h = _norm_mod(x, g_ref[...], sc_ref[...], sh_ref[...])
    h_ref[...] = _pack_rows(h)
    hh, hl = _split(h)
    rh, rl = _split(rw_ref[...])
    logits = _dot_nt(rh, hh) + _dot_nt(rl, hh) + _dot_nt(rh, hl)
    ids, w = _route(jax.nn.sigmoid(logits), rb_ref[...])
    idx_ref[...] = ids
    wts_ref[...] = w
    _, member = _membership(ids, ids.shape[1])
    cnt = jnp.sum(member, axis=1, keepdims=True)
    cnt_ref[...] = jnp.broadcast_to(cnt, cnt_ref.shape).astype(I32)
    u = _dot(hh, s13_ref[...])
    hs = _silu(u[:, :D_SHARED]) * u[:, D_SHARED:]
    xs_ref[...] = x + gtf_ref[...] * _dot(hs.astype(BF16), s2_ref[...])


def _post_mixer(parts, weights, x_pair, gt, g, sc, sh, gtf, rw_t, rb, s13, s2, bounds, split_tile, name):
    d = x_pair[0].shape[1]
    nt = bounds[-1]
    t = nt * TM
    seq = lambda i: (_seq_of_tile(i, bounds), 0, 0)
    in_specs, args = [], []
    for pair in parts:
        in_specs += _pair_specs(pair, split_tile)
        args += list(pair)
    for w in weights:
        in_specs.append(pl.BlockSpec(w.shape, lambda i: (0, 0)))
        args.append(w)
    modspec = pl.BlockSpec((None, 1, d), seq)
    full = lambda a: pl.BlockSpec(a.shape, lambda i: (0, 0))
    in_specs += _pair_specs(x_pair, split_tile) + [modspec, full(g), modspec, modspec, modspec,
                                                   full(rw_t), full(rb), full(s13), full(s2)]
    args += [*x_pair, gt, g, sc, sh, gtf, rw_t, rb, s13, s2]
    return pl.pallas_call(
        functools.partial(_post_mixer_kernel, n_parts=len(parts), split_tile=split_tile),
        grid=(nt,),
        in_specs=in_specs,
        out_specs=[pl.BlockSpec((TM, d), lambda i: (i, 0)),
                   pl.BlockSpec((TM, d // 2), lambda i: (i, 0)),
                   pl.BlockSpec((TOP_K, TM), lambda i: (0, i)),
                   pl.BlockSpec((TOP_K, TM), lambda i: (0, i)),
                   pl.BlockSpec((None, N_EXPERTS, LANES), lambda i: (i, 0, 0))],
        out_shape=[jax.ShapeDtypeStruct((t, d), F32), jax.ShapeDtypeStruct((t, d // 2), U32),
                   jax.ShapeDtypeStruct((TOP_K, t), I32), jax.ShapeDtypeStruct((TOP_K, t), F32),
                   jax.ShapeDtypeStruct((nt, N_EXPERTS, LANES), I32)],
        compiler_params=_cparams("parallel"), name=name,
    )(*args)


def _membership(ids, n_tok):
    row = lax.broadcasted_iota(I32, (N_EXPERTS, n_tok), 0)
    m = jnp.zeros((N_EXPERTS, n_tok), F32)
    for k in range(TOP_K):
        m = m + (row == ids[k:k + 1, :]).astype(F32)
    return row, m


def _dispatch_tables(cnt_tile):
    cnt = cnt_tile[:, :, 0]
    nt = cnt.shape[0]
    carry = jnp.cumsum(cnt, axis=0) - cnt
    counts = jnp.sum(cnt, axis=0)
    padded = (counts + EXPERT_CHUNK - 1) // EXPERT_CHUNK * EXPERT_CHUNK
    pends = jnp.cumsum(padded)
    pstarts = pends - padded
    base = (pstarts[None, :] + carry).astype(F32).reshape(nt, N_EXPERTS, 1)
    n_slots = nt * TM * TOP_K + N_EXPERTS * EXPERT_CHUNK
    n_chunks = n_slots // EXPERT_CHUNK
    chunk_row = jnp.arange(n_chunks, dtype=I32) * EXPERT_CHUNK
    chunk_e = jnp.minimum(jnp.sum((pends[None, :] <= chunk_row[:, None]).astype(I32), axis=1), N_EXPERTS - 1)
    n_used = (pends[-1] // EXPERT_CHUNK).astype(I32).reshape(1)
    pad_lo = (pstarts + counts).astype(I32)
    return base, chunk_e, n_used, pad_lo, pends.astype(I32), n_chunks


def _rank_kernel(ids_ref, base_ref, tri_ref, slot_ref):
    ids = ids_ref[...]
    tm = ids.shape[1]
    row, m = _membership(ids, tm)
    prefix = _dot(m.astype(BF16), tri_ref[...])
    pos = base_ref[...] + (prefix - m)
    slots = [jnp.sum(jnp.where(row == ids[k:k + 1, :], pos, 0.0), axis=0, keepdims=True) for k in range(TOP_K)]
    slot_ref[...] = jnp.concatenate(slots, axis=0).astype(I32)


def _rank(ids, base):
    k, t = ids.shape
    tri = jnp.asarray(np.triu(np.ones((TM, TM), np.float32)).astype(ml_dtypes.bfloat16))
    return pl.pallas_call(
        _rank_kernel, grid=(t // TM,),
        in_specs=[pl.BlockSpec((k, TM), lambda i: (0, i)),
                  pl.BlockSpec((None, N_EXPERTS, 1), lambda i: (i, 0, 0)),
                  pl.BlockSpec((TM, TM), lambda i: (0, 0))],
        out_specs=pl.BlockSpec((k, TM), lambda i: (0, i)),
        out_shape=jax.ShapeDtypeStruct((k, t), I32),
        compiler_params=_cparams("parallel"), name="moe_rank",
    )(ids, base, tri)


def _push_kernel(lo_ref, hi_ref, nu_ref, slot_ref, h_ref, xs_hbm, zero, sem):
    i = pl.program_id(0)
    tm = h_ref.shape[0]
    n_chunks = xs_hbm.shape[0] // EXPERT_CHUNK

    for t in range(tm):
        for k in range(TOP_K):
            pltpu.make_async_copy(h_ref.at[pl.ds(t, 1), :], xs_hbm.at[pl.ds(slot_ref[k, t], 1), :], sem).start()

    @pl.when(i == 0)
    def _():
        zero[...] = jnp.zeros_like(zero)

        def zstart(r, carry):
            pltpu.make_async_copy(zero.at[pl.ds(0, 1), :], xs_hbm.at[pl.ds(r, 1), :], sem).start()
            return carry

        def zwait(r, carry):
            pltpu.make_async_copy(zero.at[pl.ds(0, 1), :], xs_hbm.at[pl.ds(0, 1), :], sem).wait()
            return carry

        def per_expert(e, carry):
            lax.fori_loop(lo_ref[e], hi_ref[e], zstart, 0)
            lax.fori_loop(lo_ref[e], hi_ref[e], zwait, 0)
            return carry

        lax.fori_loop(0, N_EXPERTS, per_expert, 0)

        def chunk_copy(c):
            row0 = pl.multiple_of(c * EXPERT_CHUNK, EXPERT_CHUNK)
            return pltpu.make_async_copy(zero, xs_hbm.at[pl.ds(row0, EXPERT_CHUNK), :], sem)

        def cstart(c, carry):
            chunk_copy(c).start()
            return carry

        def cwait(c, carry):
            chunk_copy(c).wait()
            return carry

        lax.fori_loop(nu_ref[0], n_chunks, cstart, 0)
        lax.fori_loop(nu_ref[0], n_chunks, cwait, 0)

    for _ in range(TOP_K):
        pltpu.make_async_copy(h_ref, xs_hbm.at[pl.ds(0, tm), :], sem).wait()


def _push(slot_of, h, pad_lo, pad_hi, n_used, n_slots):
    t, d = h.shape
    nt = t // TM
    grid_spec = pltpu.PrefetchScalarGridSpec(
        num_scalar_prefetch=3,
        grid=(nt,),
        in_specs=[pl.BlockSpec((TOP_K, TM), lambda i, lo, hi, nu: (0, i), memory_space=pltpu.SMEM),
                  pl.BlockSpec((TM, d), lambda i, lo, hi, nu: (i, 0))],
        out_specs=pl.BlockSpec(memory_space=pl.ANY),
        scratch_shapes=[pltpu.VMEM((EXPERT_CHUNK, d), h.dtype), pltpu.SemaphoreType.DMA(())],
    )
    return pl.pallas_call(
        _push_kernel, grid_spec=grid_spec,
        out_shape=jax.ShapeDtypeStruct((n_slots, d), h.dtype),
        compiler_params=_cparams("arbitrary"), name="moe_push",
    )(pad_lo, pad_hi, n_used, slot_of, h)


def _expert_kernel(ce_ref, nu_ref, x_ref, w1_ref, w3_ref, w2_ref, o_ref, w1b, w3b, w2b):
    c = pl.program_id(0)
    live = c < nu_ref[0]

    @pl.when(jnp.logical_and(live, jnp.logical_or(c == 0, ce_ref[c] != ce_ref[jnp.maximum(c - 1, 0)])))
    def _():
        w1b[...] = w1_ref[...].astype(BF16)
        w3b[...] = w3_ref[...].astype(BF16)
        w2b[...] = w2_ref[...].astype(BF16)

    @pl.when(live)
    def _():
        xa, xb = _unpack_rows(x_ref[...])
        xa = xa.astype(BF16)
        xb = xb.astype(BF16)
        half = xa.shape[1]
        a = _dot(xa, w1b[:half, :]) + _dot(xb, w1b[half:, :])
        b = _dot(xa, w3b[:half, :]) + _dot(xb, w3b[half:, :])
        o_ref[...] = _pack_rows(_dot((_silu(a) * b).astype(BF16), w2b[...]))

    @pl.when(c >= nu_ref[0])
    def _():
        o_ref[...] = jnp.zeros_like(o_ref)


def _expert_ffn(xs, chunk_e, n_used, w1, w3, w2, layer):
    n_slots, dp = xs.shape
    d = 2 * dp
    n_chunks = n_slots // EXPERT_CHUNK
    live = lambda c, nu: jnp.minimum(c, nu[0] - 1)
    wmap = lambda c, ce, nu: (layer, ce[live(c, nu)], 0, 0)
    grid_spec = pltpu.PrefetchScalarGridSpec(
        num_scalar_prefetch=2,
        grid=(n_chunks,),
        in_specs=[
            pl.BlockSpec((EXPERT_CHUNK, dp), lambda c, ce, nu: (live(c, nu), 0)),
            pl.BlockSpec((None, None, d, D_EXPERT), wmap),
            pl.BlockSpec((None, None, d, D_EXPERT), wmap),
            pl.BlockSpec((None, None, D_EXPERT, d), wmap),
        ],
        out_specs=pl.BlockSpec((EXPERT_CHUNK, dp), lambda c, ce, nu: (c, 0)),
        scratch_shapes=[pltpu.VMEM((d, D_EXPERT), BF16), pltpu.VMEM((d, D_EXPERT), BF16),
                        pltpu.VMEM((D_EXPERT, d), BF16)],
    )
    return pl.pallas_call(
        _expert_kernel, grid_spec=grid_spec,
        out_shape=jax.ShapeDtypeStruct((n_slots, dp), U32),
        compiler_params=_cparams("arbitrary"), name="moe_experts",
    )(chunk_e, n_used, xs, w1, w3, w2)


def _combine_kernel(sc_ref, sn_ref, w_ref, xs_ref, gt_ref, ys_hbm, o_ref, buf, sem):
    i = pl.program_id(0)
    n = pl.num_programs(0)
    slot = i % 2

    def issue(s_ref, s):
        for r in range(TC):
            for k in range(TOP_K):
                pltpu.make_async_copy(ys_hbm.at[pl.ds(s_ref[k, r], 1), :],
                                      buf.at[s, k, pl.ds(r, 1), :], sem.at[s]).start()

    @pl.when(i == 0)
    def _():
        issue(sc_ref, 0)

    for s in range(2):
        @pl.when(jnp.logical_and(i + 1 < n, 1 - slot == s))
        def _(s=s):
            issue(sn_ref, s)

    for k in range(TOP_K):
        pltpu.make_async_copy(ys_hbm.at[pl.ds(0, TC), :], buf.at[slot, k], sem.at[slot]).wait()
    dp = buf.shape[-1]
    acc_a = acc_b = None
    for k in range(TOP_K):
        wcol = jnp.transpose(jnp.broadcast_to(w_ref[k:k + 1, :], (LANES, TC)))
        wk = _lane_tile(wcol, dp)
        ya, yb = _unpack_rows(buf[slot, k])
        acc_a = ya * wk if acc_a is None else acc_a + ya * wk
        acc_b = yb * wk if acc_b is None else acc_b + yb * wk
    o_ref[...] = xs_ref[...] + gt_ref[...] * jnp.concatenate([acc_a, acc_b], axis=-1)


def _combine(slot_of, wts, xs, gtf, ys, bounds_tc):
    t, d = xs.shape
    nt = t // TC
    seq = lambda i: (_seq_of_tile(i, bounds_tc), 0, 0)
    return pl.pallas_call(
        _combine_kernel, grid=(nt,),
        in_specs=[pl.BlockSpec((TOP_K, TC), lambda i: (0, i), memory_space=pltpu.SMEM),
                  pl.BlockSpec((TOP_K, TC), lambda i: (0, jnp.minimum(i + 1, nt - 1)), memory_space=pltpu.SMEM),
                  pl.BlockSpec((TOP_K, TC), lambda i: (0, i)),
                  pl.BlockSpec((TC, d), lambda i: (i, 0)),
                  pl.BlockSpec((None, 1, d), seq),
                  pl.BlockSpec(memory_space=pl.ANY)],
        out_specs=pl.BlockSpec((TC, d), lambda i: (i, 0)),
        out_shape=jax.ShapeDtypeStruct((t, d), F32),
        scratch_shapes=[pltpu.VMEM((2, TOP_K, TC, ys.shape[1]), ys.dtype), pltpu.SemaphoreType.DMA((2,))],
        compiler_params=_cparams("arbitrary"), name="moe_combine",
    )(slot_of, slot_of, wts, xs, gtf, ys)


def _final_norm_kernel(x_ref, g_ref, o_ref):
    x = x_ref[...]
    ms = jnp.mean(x * x, axis=-1, keepdims=True)
    o_ref[...] = x * lax.rsqrt(ms + EPS) * g_ref[...]


def _final_norm(x, g, tile_off, n_tiles):
    d = x.shape[1]
    return pl.pallas_call(
        _final_norm_kernel, grid=(n_tiles,),
        in_specs=[pl.BlockSpec((TM, d), lambda i: (i + tile_off, 0)), pl.BlockSpec((1, d), lambda i: (0, 0))],
        out_specs=pl.BlockSpec((TM, d), lambda i: (i, 0)),
        out_shape=jax.ShapeDtypeStruct((n_tiles * TM, d), F32),
        compiler_params=_cparams("parallel"), name="final_norm",
    )(x, g)


def kernel(x_prompt, x_sample, c_prompt, c_sample, ada_w, ada_b, g_mix, g_ffn, g_final,
           w_in_ab, w_out_ab, conv_w, conv_b, filt_w1, filt_b1, filt_f1, filt_w2, filt_b2, filt_f2,
           filt_w3, decay, hyena_bias, w_qkv, w_out_c, rpb,
           router_w, router_b, exp_w1, exp_w3, exp_w2, sh_w1, sh_w3, sh_w2):
    d = D_MODEL
    bp, lp, _ = x_prompt.shape
    bs, ls, _ = x_sample.shape
    trunks = ((0, bp, lp), (bp * lp, bs, ls))
    t_all = bp * lp + bs * ls
    n_seq = bp + bs
    depth = ada_w.shape[0]
    assert lp % TM == 0 and ls % TM == 0
    bounds, acc = [], 0
    for _, b, l in trunks:
        for _ in range(b):
            acc += l // TM
            bounds.append(acc)
    bounds = tuple(bounds)
    bounds_tc = tuple(b * (TM // TC) for b in bounds)
    split_tile = bp * lp // TM

    x_pair = (x_prompt.reshape(bp * lp, d), x_sample.reshape(bs * ls, d))
    c_all = jnp.concatenate([c_prompt, c_sample, jnp.zeros((SUBLANES - n_seq, d), F32)], axis=0)
    mod = _modulation(c_all, ada_w, ada_b)[:, :n_seq]
    mod = mod.reshape(depth, n_seq, 6, 1, d)

    bands = jnp.linspace(1e-4, POS_BANDS - 1, POS_BANDS, dtype=F32).reshape(1, POS_BANDS)
    bias_tab = None

    for l in range(depth):
        sh_m, sc_m, gt_m, sh_f, sc_f, gt_f = (mod[l, :, j] for j in range(6))
        g_m = g_mix[l].reshape(1, d)
        i = l // 2
        if l % 2 == 0:
            u_f, u_h = _norm_mod_matmul(x_pair, g_m, sc_m, sh_m, w_in_ab[i].astype(BF16),
                                        (D_FNET, 3 * D_HYENA), (F32, F32), bounds, split_tile, "in_proj_ab")
            z, x0 = _hyena_pre(u_h, conv_w[i], conv_b[i], bounds)
            ya, yb = [], []
            for tok_off, b, sl in trunks:
                plan = _HyenaPlan(sl)
                hc, ss = _hyena_filter(sl, bands, filt_w1[i], filt_b1[i], filt_f1[i], filt_w2[i],
                                       filt_b2[i], filt_f2[i], filt_w3[i], decay[i])
                hr, hi = _hyena_spectrum(plan, hc)
                yb.append(_hyena_longconv(plan, z, x0, hyena_bias[i], hr, hi, ss, tok_off, b, sl))
                ya.append(_fnet_mix(u_f, tok_off, b, sl))
            parts = [tuple(ya), tuple(yb)]
            w_out = w_out_ab[i].astype(BF16)
            weights = [w_out[:D_FNET], w_out[D_FNET:]]
        else:
            q, k, v = _norm_mod_matmul(x_pair, g_m, sc_m, sh_m, w_qkv[i].astype(BF16),
                                       (d, d, d), (BF16, BF16, BF16), bounds, split_tile, "qkv_proj")
            bias_tab = _attn_bias_table(rpb[i])
            o = tuple(_attention(q, k, v, bias_tab, tok_off, b, sl) for tok_off, b, sl in trunks)
            parts = [o]
            weights = [w_out_c[i].astype(BF16)]
        s13 = jnp.concatenate([sh_w1[l], sh_w3[l]], axis=1).astype(BF16)
        xs, h, idx, wts, cnt_tile = _post_mixer(
            parts, weights, x_pair, gt_m, g_ffn[l].reshape(1, d), sc_f, sh_f, gt_f,
            router_w[l].T, router_b[l].reshape(N_EXPERTS, 1), s13, sh_w2[l].astype(BF16),
            bounds, split_tile, "post_mixer_%d" % l)
        base, chunk_e, n_used, pad_lo, pad_hi, n_chunks = _dispatch_tables(cnt_tile)
        slot_of = _rank(idx, base)
        xg = _push(slot_of, h, pad_lo, pad_hi, n_used, n_chunks * EXPERT_CHUNK)
        ys = _expert_ffn(xg, chunk_e, n_used, exp_w1, exp_w3, exp_w2, l)
        x = _combine(slot_of, wts, xs, gt_f, ys, bounds_tc)
        x_pair = (x, x)

    g_fin = g_final.reshape(1, d)
    y_prompt = _final_norm(x, g_fin, 0, bp * lp // TM).reshape(bp, lp, d)
    y_sample = _final_norm(x, g_fin, bp * lp // TM, bs * ls // TM).reshape(bs, ls, d)
    return (y_prompt, y_sample)
```

```python
import functools
import math

import ml_dtypes
import numpy as np

import jax
import jax.numpy as jnp
from jax import lax
from jax.experimental import pallas as pl
from jax.experimental.pallas import tpu as pltpu

F32 = jnp.float32
BF16 = jnp.bfloat16
I32 = jnp.int32
U32 = jnp.uint32

D_MODEL = 1024
GRID_W = 64
D_FNET = 512
N_FNET_GROUPS = 4
D_GROUP = D_FNET // N_FNET_GROUPS
D_HYENA = 512
POS_BANDS = 16
N_HEADS = 16
HEAD_DIM = 64
WIN_ROWS = 8
WIN_COLS = 16
N_EXPERTS = 64
TOP_K = 8
N_GROUPS = 8
GROUP_SIZE = N_EXPERTS // N_GROUPS
TOPK_GROUPS = 4
D_EXPERT = 256
D_SHARED = 256
ROUTED_SCALE = 2.5
EXPERT_CHUNK = 256
EPS = 1e-6
NEG_BIG = -1e30

LANES = 128
SUBLANES = 8
TM = 512
TC = 128
NB = 8
VMEM_LIMIT = 56 * 1024 * 1024


def _cparams(*sem):
    return pltpu.CompilerParams(dimension_semantics=sem, vmem_limit_bytes=VMEM_LIMIT)


def _dot(a, b):
    return jnp.dot(a, b, preferred_element_type=F32)


def _dot_nt(a, b):
    return lax.dot_general(a, b, (((1,), (1,)), ((), ())), preferred_element_type=F32)


def _split(x):
    hi = x.astype(BF16)
    lo = (x - hi.astype(F32)).astype(BF16)
    return hi, lo


def _dft_l(m, x):
    return _dot(m, x.astype(BF16))


def _dft_r(x, m):
    return _dot(x.astype(BF16), m)


def _dot3(a, b):
    ah, al = _split(a)
    bh, bl = _split(b)
    return _dot(ah, bh) + _dot(al, bh) + _dot(ah, bl)


def _silu(x):
    return x * jax.nn.sigmoid(x)


def _lane_tile(t, width):
    return jnp.concatenate([t] * (width // LANES), axis=-1)


def _pack_rows(x):
    n = x.shape[1] // 2
    xr = x.astype(BF16).astype(F32)
    hi = lax.bitcast_convert_type(xr[:, :n], U32)
    lo = lax.bitcast_convert_type(xr[:, n:], U32)
    return hi | (lo >> 16)


def _unpack_rows(p):
    hi = lax.bitcast_convert_type(p & jnp.uint32(0xFFFF0000), F32)
    lo = lax.bitcast_convert_type(p << 16, F32)
    return hi, lo


def _cmul(ar, ai, br, bi):
    return ar * br - ai * bi, ar * bi + ai * br


def _norm_mod(x, g, sc, sh):
    ms = jnp.mean(x * x, axis=-1, keepdims=True)
    return (x * lax.rsqrt(ms + EPS) * g) * (1.0 + sc) + sh


def _seq_of_tile(i, bounds):
    s = jnp.int32(0)
    for b in bounds[:-1]:
        s = s + (i >= b).astype(I32)
    return s


def _bf16c(m):
    return jnp.asarray(np.asarray(m, np.float32).astype(ml_dtypes.bfloat16))


def _cos_sin(n):
    j = np.arange(n)
    ang = 2.0 * np.pi * ((j[:, None] * j[None, :]) % n) / n
    return np.cos(ang), np.sin(ang)


def _twiddle(n_slab, n_row, n_total):
    s = jnp.arange(n_slab, dtype=I32)[:, None]
    r = jnp.arange(n_row, dtype=I32)[None, :]
    ang = ((s * r) % n_total).astype(F32) * (2.0 * math.pi / n_total)
    shape = (n_slab, n_row, LANES)
    return (jnp.broadcast_to(jnp.cos(ang)[:, :, None], shape),
            jnp.broadcast_to(-jnp.sin(ang)[:, :, None], shape))


def _mod_kernel(c_ref, w_ref, b_ref, o_ref):
    o_ref[...] = _dot3(_silu(c_ref[...]), w_ref[...]) + b_ref[...]


def _modulation(c_pad, ada_w, ada_b):
    depth, d, n = ada_w.shape
    tn = 1536
    return pl.pallas_call(
        _mod_kernel,
        grid=(depth, n // tn),
        in_specs=[pl.BlockSpec((SUBLANES, d), lambda l, j: (0, 0)),
                  pl.BlockSpec((None, d, tn), lambda l, j: (l, 0, j)),
                  pl.BlockSpec((None, 1, tn), lambda l, j: (l, 0, j))],
        out_specs=pl.BlockSpec((None, SUBLANES, tn), lambda l, j: (l, 0, j)),
        out_shape=jax.ShapeDtypeStruct((depth, SUBLANES, n), F32),
        compiler_params=_cparams("parallel", "parallel"),
        name="adaln_mod",
    )(c_pad, ada_w, ada_b.reshape(depth, 1, n))


def _pair_specs(pair, split_tile):
    a, b = pair
    w = a.shape[1]
    b_off = 0 if a is b else split_tile
    return [pl.BlockSpec((TM, w), lambda i: (jnp.minimum(i, split_tile - 1), 0)),
            pl.BlockSpec((TM, w), lambda i: (jnp.maximum(i, split_tile) - b_off, 0))]


def _pair_read(a_ref, b_ref, split_tile):
    return jnp.where(pl.program_id(0) < split_tile, a_ref[...], b_ref[...])


def _nmm_kernel(xa_ref, xb_ref, g_ref, sc_ref, sh_ref, w_ref, *o_refs, split_tile):
    x = _pair_read(xa_ref, xb_ref, split_tile)
    h = _norm_mod(x, g_ref[...], sc_ref[...], sh_ref[...])
    u = _dot(h.astype(BF16), w_ref[...])
    off = 0
    for o in o_refs:
        n = o.shape[-1]
        o[...] = u[:, off:off + n].astype(o.dtype)
        off += n


def _norm_mod_matmul(x_pair, g, sc, sh, w_bf16, splits, out_dtypes, bounds, split_tile, name):
    d, n = w_bf16.shape
    nt = bounds[-1]
    seq = lambda i: (_seq_of_tile(i, bounds), 0, 0)
    return pl.pallas_call(
        functools.partial(_nmm_kernel, split_tile=split_tile),
        grid=(nt,),
        in_specs=_pair_specs(x_pair, split_tile) + [
            pl.BlockSpec((1, d), lambda i: (0, 0)),
            pl.BlockSpec((None, 1, d), seq),
            pl.BlockSpec((None, 1, d), seq),
            pl.BlockSpec((d, n), lambda i: (0, 0))],
        out_specs=[pl.BlockSpec((TM, s), lambda i: (i, 0)) for s in splits],
        out_shape=[jax.ShapeDtypeStruct((nt * TM, s), dt) for s, dt in zip(splits, out_dtypes)],
        compiler_params=_cparams("parallel"),
        name=name,
    )(*x_pair, g, sc, sh, w_bf16)


def _hyena_pre_kernel(prev_ref, cur_ref, next_ref, cw_ref, cb_ref, z_ref, x0_ref, *, first_tiles, last_tiles):
    i = pl.program_id(0)
    cur = cur_ref[...]
    tm = cur.shape[0]
    is_first = functools.reduce(jnp.logical_or, [i == f for f in first_tiles])
    is_last = functools.reduce(jnp.logical_or, [i == f for f in last_tiles])
    prow = jnp.where(is_first, 0.0, prev_ref[SUBLANES - 1:SUBLANES, :])
    nrow = jnp.where(is_last, 0.0, next_ref[0:1, :])
    rid = lax.broadcasted_iota(I32, (tm, 1), 0)
    up = jnp.where(rid == 0, prow, pltpu.roll(cur, 1, 0))
    dn = jnp.where(rid == tm - 1, nrow, pltpu.roll(cur, tm - 1, 0))
    cw = cw_ref[...]
    uc = cb_ref[...] + (up * cw[0:1, :] + cur * cw[1:2, :] + dn * cw[2:3, :])
    x0_ref[...] = uc[:, :D_HYENA]
    z_ref[...] = uc[:, 2 * D_HYENA:] * uc[:, D_HYENA:2 * D_HYENA]


def _hyena_pre(u_h, conv_w, conv_b, bounds):
    t, c = u_h.shape
    nt = t // TM
    rb = TM // SUBLANES
    first_tiles = (0,) + tuple(bounds[:-1])
    last_tiles = tuple(b - 1 for b in bounds)
    kern = functools.partial(_hyena_pre_kernel, first_tiles=first_tiles, last_tiles=last_tiles)
    return pl.pallas_call(
        kern,
        grid=(nt,),
        in_specs=[pl.BlockSpec((SUBLANES, c), lambda i: (jnp.maximum(i * rb - 1, 0), 0)),
                  pl.BlockSpec((TM, c), lambda i: (i, 0)),
                  pl.BlockSpec((SUBLANES, c), lambda i: (jnp.minimum((i + 1) * rb, nt * rb - 1), 0)),
                  pl.BlockSpec((3, c), lambda i: (0, 0)),
                  pl.BlockSpec((1, c), lambda i: (0, 0))],
        out_specs=[pl.BlockSpec((TM, D_HYENA), lambda i: (i, 0)),
                   pl.BlockSpec((TM, D_HYENA), lambda i: (i, 0))],
        out_shape=[jax.ShapeDtypeStruct((t, D_HYENA), F32), jax.ShapeDtypeStruct((t, D_HYENA), F32)],
        compiler_params=_cparams("parallel"),
        name="hyena_pre",
    )(u_h, u_h, u_h, conv_w, conv_b.reshape(1, c))


def _filter_kernel(bands_ref, w1_ref, b1_ref, f1_ref, w2_ref, b2_ref, f2_ref, w3_ref, dec_ref,
                   hc_ref, ss_ref, *, seq_len):
    i = pl.program_id(0)
    tm = hc_ref.shape[0]
    is_bwd = i >= seq_len // tm

    def offset_of(row):
        return jnp.where(is_bwd, 2 * seq_len - row, row).astype(F32)

    n = offset_of(i * tm + lax.broadcasted_iota(I32, (1, tm), 1))
    t = n / float(seq_len - 1)
    ang = (2.0 * math.pi / seq_len) * bands_ref[...] * n
    pre = (w1_ref[:, 0:1] * t + _dot3(w1_ref[:, 1:1 + POS_BANDS], jnp.cos(ang))
           + _dot3(w1_ref[:, 1 + POS_BANDS:], -jnp.sin(ang)) + b1_ref[...])
    h = jnp.sin(f1_ref[...] * pre)
    h = jnp.sin(f2_ref[...] * (_dot3(w2_ref[...], h) + b2_ref[...]))
    w3 = w3_ref[...]
    w3 = jnp.where(is_bwd, w3[:, D_HYENA:], w3[:, :D_HYENA])
    hh, hl = _split(h)
    wh, wl = _split(w3)
    tn = lambda a, b: lax.dot_general(a, b, (((0,), (0,)), ((), ())), preferred_element_type=F32)
    h3 = tn(hh, wh) + tn(hl, wh) + tn(hh, wl)
    row = i * tm + lax.broadcasted_iota(I32, (tm, 1), 0)
    t_col = offset_of(row) / float(seq_len - 1)
    dec = jnp.abs(dec_ref[...])
    hc = h3 * jnp.exp(-t_col * jnp.where(is_bwd, dec[1:2, :], dec[0:1, :]))
    hc = jnp.where(row == seq_len, 0.0, hc)
    hc_ref[...] = hc

    @pl.when(i == 0)
    def _():
        ss_ref[...] = jnp.zeros_like(ss_ref)

    ss_ref[...] += jnp.sum(hc * hc, axis=0, keepdims=True)


def _hyena_filter(seq_len, bands, w1, b1, f1, w2, b2, f2, w3, decay):
    tm = 512
    full = lambda a: pl.BlockSpec(a.shape, lambda i: (0,) * a.ndim)
    col = lambda a: a.reshape(-1, 1)
    args = (col(bands), w1.T, col(b1), col(f1), w2.T, col(b2), col(f2), w3, decay)
    return pl.pallas_call(
        functools.partial(_filter_kernel, seq_len=seq_len),
        grid=(2 * seq_len // tm,),
        in_specs=[full(a) for a in args],
        out_specs=[pl.BlockSpec((tm, D_HYENA), lambda i: (i, 0)),
                   pl.BlockSpec((1, D_HYENA), lambda i: (0, 0))],
        out_shape=[jax.ShapeDtypeStruct((2 * seq_len, D_HYENA), F32),
                   jax.ShapeDtypeStruct((1, D_HYENA), F32)],
        compiler_params=_cparams("arbitrary"),
        name="hyena_filter",
    )(*args)


def _strided_fwd_kernel(x_ref, m_ref, or_ref, oi_ref):
    n1 = or_ref.shape[0]
    m = m_ref[...]
    for j in range(NB):
        a = _dft_l(m, x_ref[:, j, :])
        or_ref[:, j, :] = a[:n1]
        oi_ref[:, j, :] = a[n1:]


def _strided_fwd(x3, row_block, rows_in, m, batch, name):
    _, n2, ch = x3.shape
    n1 = m.shape[0] // 2
    out = jax.ShapeDtypeStruct((batch, n1, n2, ch), F32)
    ospec = pl.BlockSpec((None, n1, NB, ch), lambda b, j: (b, 0, j, 0))
    return pl.pallas_call(
        _strided_fwd_kernel,
        grid=(batch, n2 // NB),
        in_specs=[pl.BlockSpec((rows_in, NB, ch), lambda b, j: (row_block + b, j, 0)),
                  pl.BlockSpec(m.shape, lambda b, j: (0, 0))],
        out_specs=[ospec, ospec],
        out_shape=[out, out],
        compiler_params=_cparams("parallel", "parallel"),
        name=name,
    )(x3, m)


def _slab_specs(n1, n2, ch, batched):
    if batched:
        return pl.BlockSpec((None, NB, n2, ch), lambda b, k: (b, k, 0, 0))
    return pl.BlockSpec((None, NB, n2, ch), lambda b, k: (0, k, 0, 0))


def _slab_fwd_kernel(ar_ref, ai_ref, twr_ref, twi_ref, f_ref, or_ref, oi_ref):
    n2, ch = ar_ref.shape[1:]
    f = f_ref[...]
    for s in range(NB):
        twr = _lane_tile(twr_ref[s], ch)
        twi = _lane_tile(twi_ref[s], ch)
        br, bi = _cmul(ar_ref[s], ai_ref[s], twr, twi)
        x = _dft_l(f, jnp.concatenate([br, bi], axis=0))
        or_ref[s] = x[:n2]
        oi_ref[s] = x[n2:]


def _slab_conv_kernel(ar_ref, ai_ref, hr_ref, hi_ref, ss_ref, twr_ref, twi_ref, f_ref, g_ref, or_ref, oi_ref):
    n2, ch = ar_ref.shape[1:]
    f = f_ref[...]
    gm = g_ref[...]
    scale = lax.rsqrt(ss_ref[...] + EPS)
    for s in range(NB):
        twr = _lane_tile(twr_ref[s], ch)
        twi = _lane_tile(twi_ref[s], ch)
        br, bi = _cmul(ar_ref[s], ai_ref[s], twr, twi)
        x = _dft_l(f, jnp.concatenate([br, bi], axis=0))
        pr, pi = _cmul(x[:n2], x[n2:], hr_ref[s] * scale, hi_ref[s] * scale)
        g = _dft_l(gm, jnp.concatenate([pr, pi], axis=0))
        qr, qi = _cmul(g[:n2], g[n2:], twr, -twi)
        or_ref[s] = qr
        oi_ref[s] = qi


def _hyena_out_kernel(gr_ref, gi_ref, z_ref, x0_ref, hb_ref, m_ref, o_ref):
    m = m_ref[...]
    hb = hb_ref[...]
    for j in range(NB):
        g = jnp.concatenate([gr_ref[:, j, :], gi_ref[:, j, :]], axis=0)
        zc = _dft_l(m, g)
        o_ref[:, j, :] = x0_ref[:, j, :] * (zc + z_ref[:, j, :] * hb)


class _HyenaPlan:
    def __init__(self, seq_len):
        n = 2 * seq_len
        self.n = n
        self.n1 = 256 if n >= 32768 else 128
        self.n2 = n // self.n1
        self.half = self.n1 // 2
        c1, s1 = _cos_sin(self.n1)
        c2, s2 = _cos_sin(self.n2)
        h = self.half
        self.m1 = _bf16c(np.concatenate([c1, -s1], axis=0))
        self.m1_half = _bf16c(np.concatenate([c1[:, :h], -s1[:, :h]], axis=0))
        self.f2 = _bf16c(np.block([[c2, s2], [-s2, c2]]))
        self.g2 = _bf16c(np.block([[c2, -s2], [s2, c2]]) / n)
        self.m2 = _bf16c(np.concatenate([c1[:h], -s1[:h]], axis=1))


def _hyena_spectrum(plan, hc):
    n1, n2 = plan.n1, plan.n2
    ch = hc.shape[1]
    ar, ai = _strided_fwd(hc.reshape(n1, n2, ch), 0, n1, plan.m1, 1, "hyena_filt_s1")
    twr, twi = _twiddle(n1, n2, plan.n)
    spec = _slab_specs(n1, n2, ch, True)
    tspec = pl.BlockSpec((NB, n2, LANES), lambda b, k: (k, 0, 0))
    cspec = pl.BlockSpec(plan.f2.shape, lambda b, k: (0, 0))
    out = jax.ShapeDtypeStruct((1, n1, n2, ch), F32)
    return pl.pallas_call(
        _slab_fwd_kernel, grid=(1, n1 // NB),
        in_specs=[spec, spec, tspec, tspec, cspec],
        out_specs=[spec, spec], out_shape=[out, out],
        compiler_params=_cparams("parallel", "parallel"), name="hyena_filt_slab",
    )(ar, ai, twr, twi, plan.f2)


def _hyena_longconv(plan, z, x0, hyena_bias, hr, hi, ss, tok_off, batch, seq_len):
    n1, n2, half = plan.n1, plan.n2, plan.half
    t, ch = z.shape
    assert tok_off % seq_len == 0 and seq_len == half * n2
    row_block = tok_off // seq_len
    z3 = z.reshape(t // n2, n2, ch)
    x03 = x0.reshape(t // n2, n2, ch)
    ar, ai = _strided_fwd(z3, row_block, half, plan.m1_half, batch, "hyena_s1")
    twr, twi = _twiddle(n1, n2, plan.n)
    spec = _slab_specs(n1, n2, ch, True)
    hspec = _slab_specs(n1, n2, ch, False)
    tspec = pl.BlockSpec((NB, n2, LANES), lambda b, k: (k, 0, 0))
    cspec = pl.BlockSpec(plan.f2.shape, lambda b, k: (0, 0))
    out = jax.ShapeDtypeStruct((batch, n1, n2, ch), F32)
    gr, gi = pl.pallas_call(
        _slab_conv_kernel, grid=(batch, n1 // NB),
        in_specs=[spec, spec, hspec, hspec, pl.BlockSpec((1, ch), lambda b, k: (0, 0)),
                  tspec, tspec, cspec, cspec],
        out_specs=[spec, spec], out_shape=[out, out],
        compiler_params=_cparams("parallel", "parallel"), name="hyena_slab_conv",
    )(ar, ai, hr, hi, ss, twr, twi, plan.f2, plan.g2)
    gspec = pl.BlockSpec((None, n1, NB, ch), lambda b, j: (b, 0, j, 0))
    xspec = pl.BlockSpec((half, NB, ch), lambda b, j: (row_block + b, j, 0))
    y = pl.pallas_call(
        _hyena_out_kernel, grid=(batch, n2 // NB),
        in_specs=[gspec, gspec, xspec, xspec, pl.BlockSpec((1, ch), lambda b, j: (0, 0)),
                  pl.BlockSpec(plan.m2.shape, lambda b, j: (0, 0))],
        out_specs=pl.BlockSpec((half, NB, ch), lambda b, j: (b, j, 0)),
        out_shape=jax.ShapeDtypeStruct((batch * half, n2, ch), F32),
        compiler_params=_cparams("parallel", "parallel"), name="hyena_s2",
    )(gr, gi, z3, x03, hyena_bias.reshape(1, ch), plan.m2)
    return y.reshape(batch * seq_len, ch)


def _fnet_s1_kernel(x_ref, w_ref, m_ref, or_ref, oi_ref):
    n1 = or_ref.shape[0]
    w = w_ref[...]
    m = m_ref[...]
    for j in range(NB):
        x = x_ref[:, j, :]
        zr, zi = [], []
        for g in range(N_FNET_GROUPS):
            zg = _dft_r(x[:, g * D_GROUP:(g + 1) * D_GROUP], w)
            zr.append(zg[:, :D_GROUP])
            zi.append(zg[:, D_GROUP:])
        z = jnp.concatenate([jnp.concatenate(zr, axis=1), jnp.concatenate(zi, axis=1)], axis=0)
        a = _dft_l(m, z)
        or_ref[:, j, :] = a[:n1]
        oi_ref[:, j, :] = a[n1:]


def _fnet_slab_kernel(ar_ref, ai_ref, twr_ref, twi_ref, f_ref, o_ref, *, scale):
    ch = ar_ref.shape[2]
    f = f_ref[...]
    for s in range(NB):
        twr = _lane_tile(twr_ref[s], ch)
        twi = _lane_tile(twi_ref[s], ch)
        br, bi = _cmul(ar_ref[s], ai_ref[s], twr, twi)
        o_ref[:, s, :] = _dft_l(f, jnp.concatenate([br, bi], axis=0)) * scale


def _fnet_mix(u_f, tok_off, batch, seq_len):
    t, ch = u_f.shape
    n2 = 128
    n1 = seq_len // n2
    assert tok_off % seq_len == 0
    row_block = tok_off // seq_len
    c1, s1 = _cos_sin(n1)
    c2, s2 = _cos_sin(n2)
    cg, sg = _cos_sin(D_GROUP)
    w = _bf16c(np.concatenate([cg, -sg], axis=1))
    m = _bf16c(np.block([[c1, s1], [-s1, c1]]))
    f = _bf16c(np.concatenate([c2, s2], axis=1))
    u3 = u_f.reshape(t // n2, n2, ch)
    out = jax.ShapeDtypeStruct((batch, n1, n2, ch), F32)
    ospec = pl.BlockSpec((None, n1, NB, ch), lambda b, j: (b, 0, j, 0))
    c2d = lambda a: pl.BlockSpec(a.shape, lambda b, j: (0, 0))
    ar, ai = pl.pallas_call(
        _fnet_s1_kernel, grid=(batch, n2 // NB),
        in_specs=[pl.BlockSpec((n1, NB, ch), lambda b, j: (row_block + b, j, 0)), c2d(w), c2d(m)],
        out_specs=[ospec, ospec], out_shape=[out, out],
        compiler_params=_cparams("parallel", "parallel"), name="fnet_s1",
    )(u3, w, m)
    twr, twi = _twiddle(n1, n2, seq_len)
    spec = _slab_specs(n1, n2, ch, True)
    tspec = pl.BlockSpec((NB, n2, LANES), lambda b, k: (k, 0, 0))
    scale = 1.0 / math.sqrt(seq_len * D_GROUP)
    y = pl.pallas_call(
        functools.partial(_fnet_slab_kernel, scale=scale), grid=(batch, n1 // NB),
        in_specs=[spec, spec, tspec, tspec, c2d(f)],
        out_specs=pl.BlockSpec((None, n2, NB, ch), lambda b, k: (b, 0, k, 0)),
        out_shape=jax.ShapeDtypeStruct((batch, n2, n1, ch), F32),
        compiler_params=_cparams("parallel", "parallel"), name="fnet_slab",
    )(ar, ai, twr, twi, f)
    return y.reshape(batch * seq_len, ch)


def _attn_kernel(q_ref, k_ref, v_ref, b_ref, o_ref):
    scale = HEAD_DIM ** -0.5
    nq = q_ref.shape[0]
    kw = k_ref.shape[0]
    low = lax.broadcasted_iota(I32, (nq, LANES), 1) < HEAD_DIM
    outs = []
    for p in range(N_HEADS // 2):
        sl = slice(p * LANES, (p + 1) * LANES)
        q2 = q_ref[:, sl]
        zero = jnp.zeros_like(q2)
        qq = jnp.concatenate([jnp.where(low, q2, zero), jnp.where(low, zero, q2)], axis=0)
        s = _dot_nt(qq, k_ref[:, sl]) * scale + b_ref[2 * p:2 * p + 2].reshape(2 * nq, kw)
        m = jnp.max(s, axis=-1, keepdims=True)
        e = jnp.exp(s - m)
        l = jnp.sum(e, axis=-1, keepdims=True)
        o = _dot(e.astype(BF16), v_ref[:, sl]) / l
        outs.append(jnp.where(low, o[:nq], o[nq:]))
    o_ref[...] = jnp.concatenate(outs, axis=-1).astype(o_ref.dtype)


def _attn_bias_table(rpb):
    cols = jnp.arange(GRID_W)
    col_start = jnp.clip(cols - WIN_COLS // 2, 0, GRID_W - WIN_COLS)
    kc = jnp.arange(GRID_W)
    inside = (kc[None, :] >= col_start[:, None]) & (kc[None, :] < col_start[:, None] + WIN_COLS)
    cidx = jnp.clip(kc[None, :] - cols[:, None] + (WIN_COLS - 1), 0, 2 * WIN_COLS - 2)
    dd = jnp.arange(WIN_ROWS)
    ridx = (WIN_ROWS - 1) - dd[:, None] + jnp.arange(WIN_ROWS)[None, :]
    tab = rpb.astype(F32)[:, ridx][:, :, :, cidx]
    tab = jnp.where(inside[None, None, None], tab, NEG_BIG)
    tab = tab.transpose(1, 0, 3, 2, 4)
    return tab.reshape(WIN_ROWS, N_HEADS, GRID_W, WIN_ROWS * GRID_W)


def _attention(q, k, v, bias_tab, tok_off, batch, seq_len):
    rows = seq_len // GRID_W
    assert rows >= WIN_ROWS
    d = q.shape[1]
    kw = WIN_ROWS * GRID_W

    def rs_of(r):
        return jnp.clip(r - WIN_ROWS // 2, 0, rows - WIN_ROWS)

    kspec = pl.BlockSpec((pl.Element(kw), pl.Element(d)),
                         lambda b, r: (pl.multiple_of(tok_off + b * seq_len + rs_of(r) * GRID_W, GRID_W), 0))
    return pl.pallas_call(
        _attn_kernel, grid=(batch, rows),
        in_specs=[pl.BlockSpec((GRID_W, d), lambda b, r: (tok_off // GRID_W + b * rows + r, 0)),
                  kspec, kspec,
                  pl.BlockSpec((None, N_HEADS, GRID_W, kw), lambda b, r: (r - rs_of(r), 0, 0, 0))],
        out_specs=pl.BlockSpec((GRID_W, d), lambda b, r: (b * rows + r, 0)),
        out_shape=jax.ShapeDtypeStruct((batch * seq_len, d), BF16),
        compiler_params=_cparams("parallel", "parallel"), name="natten",
    )(q, k, v, bias_tab)


def _route(scores, rb):
    sel = scores + rb
    tm = sel.shape[1]
    sub = lax.broadcasted_iota(I32, (GROUP_SIZE, tm), 0)
    ninf = -jnp.inf
    groups = []
    for g in range(N_GROUPS):
        sg = sel[g * GROUP_SIZE:(g + 1) * GROUP_SIZE, :]
        m1 = jnp.max(sg, axis=0, keepdims=True)
        i1 = jnp.min(jnp.where(sg == m1, sub, GROUP_SIZE), axis=0, keepdims=True)
        m2 = jnp.max(jnp.where(sub == i1, ninf, sg), axis=0, keepdims=True)
        groups.append(m1 + m2)
    gs = jnp.concatenate(groups, axis=0)
    gsel = jnp.zeros((N_GROUPS, tm), F32)
    for _ in range(TOPK_GROUPS):
        m = jnp.max(gs, axis=0, keepdims=True)
        gi = jnp.min(jnp.where(gs == m, sub, N_GROUPS), axis=0, keepdims=True)
        hit = sub == gi
        gsel = jnp.where(hit, 1.0, gsel)
        gs = jnp.where(hit, ninf, gs)
    masked = jnp.concatenate(
        [jnp.where(gsel[g:g + 1, :] > 0.0, sel[g * GROUP_SIZE:(g + 1) * GROUP_SIZE, :], ninf)
         for g in range(N_GROUPS)], axis=0)
    row = lax.broadcasted_iota(I32, (N_EXPERTS, tm), 0)
    ids, ws = [], []
    for _ in range(TOP_K):
        m = jnp.max(masked, axis=0, keepdims=True)
        ii = jnp.min(jnp.where(masked == m, row, N_EXPERTS), axis=0, keepdims=True)
        hit = row == ii
        ws.append(jnp.sum(jnp.where(hit, scores, 0.0), axis=0, keepdims=True))
        ids.append(ii)
        masked = jnp.where(hit, ninf, masked)
    w = jnp.concatenate(ws, axis=0)
    w = w / jnp.sum(w, axis=0, keepdims=True) * ROUTED_SCALE
    return jnp.concatenate(ids, axis=0), w


def _post_mixer_kernel(*refs, n_parts, split_tile):
    ys = refs[:2 * n_parts]
    ws = refs[2 * n_parts:3 * n_parts]
    (xa_ref, xb_ref, gt_ref, g_ref, sc_ref, sh_ref, gtf_ref, rw_ref, rb_ref, s13_ref, s2_ref,
     xs_ref, h_ref, idx_ref, wts_ref, cnt_ref) = refs[3 * n_parts:]
    acc = None
    for p in range(n_parts):
        y = _pair_read(ys[2 * p], ys[2 * p + 1], split_tile)
        d = _dot(y.astype(BF16), ws[p][...])
        acc = d if acc is None else acc + d
    x = _pair_read(xa_ref, xb_ref, split_tile) + gt_ref[...] * acc
    h = _norm_mod(x, g_ref[...], sc_ref[...], sh_ref[...])
    h_ref[...] = _pack_rows(h)
    hh, hl = _split(h)
    rh, rl = _split(rw_ref[...])
    logits = _dot_nt(rh, hh) + _dot_nt(rl, hh) + _dot_nt(rh, hl)
    ids, w = _route(jax.nn.sigmoid(logits), rb_ref[...])
    idx_ref[...] = ids
    wts_ref[...] = w
    _, member = _membership(ids, ids.shape[1])
    cnt = jnp.sum(member, axis=1, keepdims=True)
    cnt_ref[...] = jnp.broadcast_to(cnt, cnt_ref.shape).astype(I32)
    u = _dot(hh, s13_ref[...])
    hs = _silu(u[:, :D_SHARED]) * u[:, D_SHARED:]
    xs_ref[...] = x + gtf_ref[...] * _dot(hs.astype(BF16), s2_ref[...])


def _post_mixer(parts, weights, x_pair, gt, g, sc, sh, gtf, rw_t, rb, s13, s2, bounds, split_tile, name):
    d = x_pair[0].shape[1]
    nt = bounds[-1]
    t = nt * TM
    seq = lambda i: (_seq_of_tile(i, bounds), 0, 0)
    in_specs, args = [], []
    for pair in parts:
        in_specs += _pair_specs(pair, split_tile)
        args += list(pair)
    for w in weights:
        in_specs.append(pl.BlockSpec(w.shape, lambda i: (0, 0)))
        args.append(w)
    modspec = pl.BlockSpec((None, 1, d), seq)
    full = lambda a: pl.BlockSpec(a.shape, lambda i: (0, 0))
    in_specs += _pair_specs(x_pair, split_tile) + [modspec, full(g), modspec, modspec, modspec,
                                                   full(rw_t), full(rb), full(s13), full(s2)]
    args += [*x_pair, gt, g, sc, sh, gtf, rw_t, rb, s13, s2]
    return pl.pallas_call(
        functools.partial(_post_mixer_kernel, n_parts=len(parts), split_tile=split_tile),
        grid=(nt,),
        in_specs=in_specs,
        out_specs=[pl.BlockSpec((TM, d), lambda i: (i, 0)),
                   pl.BlockSpec((TM, d // 2), lambda i: (i, 0)),
                   pl.BlockSpec((TOP_K, TM), lambda i: (0, i)),
                   pl.BlockSpec((TOP_K, TM), lambda i: (0, i)),
                   pl.BlockSpec((None, N_EXPERTS, LANES), lambda i: (i, 0, 0))],
        out_shape=[jax.ShapeDtypeStruct((t, d), F32), jax.ShapeDtypeStruct((t, d // 2), U32),
                   jax.ShapeDtypeStruct((TOP_K, t), I32), jax.ShapeDtypeStruct((TOP_K, t), F32),
                   jax.ShapeDtypeStruct((nt, N_EXPERTS, LANES), I32)],
        compiler_params=_cparams("parallel"), name=name,
    )(*args)


def _membership(ids, n_tok):
    row = lax.broadcasted_iota(I32, (N_EXPERTS, n_tok), 0)
    m = jnp.zeros((N_EXPERTS, n_tok), F32)
    for k in range(TOP_K):
        m = m + (row == ids[k:k + 1, :]).astype(F32)
    return row, m


def _dispatch_tables(cnt_tile):
    cnt = cnt_tile[:, :, 0]
    nt = cnt.shape[0]
    carry = jnp.cumsum(cnt, axis=0) - cnt
    counts = jnp.sum(cnt, axis=0)
    padded = (counts + EXPERT_CHUNK - 1) // EXPERT_CHUNK * EXPERT_CHUNK
    pends = jnp.cumsum(padded)
    pstarts = pends - padded
    base = (pstarts[None, :] + carry).astype(F32).reshape(nt, N_EXPERTS, 1)
    n_slots = nt * TM * TOP_K + N_EXPERTS * EXPERT_CHUNK
    n_chunks = n_slots // EXPERT_CHUNK
    chunk_row = jnp.arange(n_chunks, dtype=I32) * EXPERT_CHUNK
    chunk_e = jnp.minimum(jnp.sum((pends[None, :] <= chunk_row[:, None]).astype(I32), axis=1), N_EXPERTS - 1)
    n_used = (pends[-1] // EXPERT_CHUNK).astype(I32).reshape(1)
    pad_lo = (pstarts + counts).astype(I32)
    return base, chunk_e, n_used, pad_lo, pends.astype(I32), n_chunks


def _rank_kernel(ids_ref, base_ref, tri_ref, slot_ref):
    ids = ids_ref[...]
    tm = ids.shape[1]
    row, m = _membership(ids, tm)
    prefix = _dot(m.astype(BF16), tri_ref[...])
    pos = base_ref[...] + (prefix - m)
    slots = [jnp.sum(jnp.where(row == ids[k:k + 1, :], pos, 0.0), axis=0, keepdims=True) for k in range(TOP_K)]
    slot_ref[...] = jnp.concatenate(slots, axis=0).astype(I32)


def _rank(ids, base):
    k, t = ids.shape
    tri = jnp.asarray(np.triu(np.ones((TM, TM), np.float32)).astype(ml_dtypes.bfloat16))
    return pl.pallas_call(
        _rank_kernel, grid=(t // TM,),
        in_specs=[pl.BlockSpec((k, TM), lambda i: (0, i)),
                  pl.BlockSpec((None, N_EXPERTS, 1), lambda i: (i, 0, 0)),
                  pl.BlockSpec((TM, TM), lambda i: (0, 0))],
        out_specs=pl.BlockSpec((k, TM), lambda i: (0, i)),
        out_shape=jax.ShapeDtypeStruct((k, t), I32),
        compiler_params=_cparams("parallel"), name="moe_rank",
    )(ids, base, tri)


def _push_kernel(lo_ref, hi_ref, nu_ref, slot_ref, h_ref, xs_hbm, zero, sem):
    i = pl.program_id(0)
    tm = h_ref.shape[0]
    n_chunks = xs_hbm.shape[0] // EXPERT_CHUNK

    for t in range(tm):
        for k in range(TOP_K):
            pltpu.make_async_copy(h_ref.at[pl.ds(t, 1), :], xs_hbm.at[pl.ds(slot_ref[k, t], 1), :], sem).start()

    @pl.when(i == 0)
    def _():
        zero[...] = jnp.zeros_like(zero)

        def zstart(r, carry):
            pltpu.make_async_copy(zero.at[pl.ds(0, 1), :], xs_hbm.at[pl.ds(r, 1), :], sem).start()
            return carry

        def zwait(r, carry):
            pltpu.make_async_copy(zero.at[pl.ds(0, 1), :], xs_hbm.at[pl.ds(0, 1), :], sem).wait()
            return carry

        def per_expert(e, carry):
            lax.fori_loop(lo_ref[e], hi_ref[e], zstart, 0)
            lax.fori_loop(lo_ref[e], hi_ref[e], zwait, 0)
            return carry

        lax.fori_loop(0, N_EXPERTS, per_expert, 0)

        def chunk_copy(c):
            row0 = pl.multiple_of(c * EXPERT_CHUNK, EXPERT_CHUNK)
            return pltpu.make_async_copy(zero, xs_hbm.at[pl.ds(row0, EXPERT_CHUNK), :], sem)

        def cstart(c, carry):
            chunk_copy(c).start()
            return carry

        def cwait(c, carry):
            chunk_copy(c).wait()
            return carry

        lax.fori_loop(nu_ref[0], n_chunks, cstart, 0)
        lax.fori_loop(nu_ref[0], n_chunks, cwait, 0)

    for _ in range(TOP_K):
        pltpu.make_async_copy(h_ref, xs_hbm.at[pl.ds(0, tm), :], sem).wait()


def _push(slot_of, h, pad_lo, pad_hi, n_used, n_slots):
    t, d = h.shape
    nt = t // TM
    grid_spec = pltpu.PrefetchScalarGridSpec(
        num_scalar_prefetch=3,
        grid=(nt,),
        in_specs=[pl.BlockSpec((TOP_K, TM), lambda i, lo, hi, nu: (0, i), memory_space=pltpu.SMEM),
                  pl.BlockSpec((TM, d), lambda i, lo, hi, nu: (i, 0))],
        out_specs=pl.BlockSpec(memory_space=pl.ANY),
        scratch_shapes=[pltpu.VMEM((EXPERT_CHUNK, d), h.dtype), pltpu.SemaphoreType.DMA(())],
    )
    return pl.pallas_call(
        _push_kernel, grid_spec=grid_spec,
        out_shape=jax.ShapeDtypeStruct((n_slots, d), h.dtype),
        compiler_params=_cparams("arbitrary"), name="moe_push",
    )(pad_lo, pad_hi, n_used, slot_of, h)


def _expert_kernel(ce_ref, nu_ref, x_ref, w1_ref, w3_ref, w2_ref, o_ref, w1b, w3b, w2b):
    c = pl.program_id(0)
    live = c < nu_ref[0]

    @pl.when(jnp.logical_and(live, jnp.logical_or(c == 0, ce_ref[c] != ce_ref[jnp.maximum(c - 1, 0)])))
    def _():
        w1b[...] = w1_ref[...].astype(BF16)
        w3b[...] = w3_ref[...].astype(BF16)
        w2b[...] = w2_ref[...].astype(BF16)

    @pl.when(live)
    def _():
        xa, xb = _unpack_rows(x_ref[...])
        xa = xa.astype(BF16)
        xb = xb.astype(BF16)
        half = xa.shape[1]
        a = _dot(xa, w1b[:half, :]) + _dot(xb, w1b[half:, :])
        b = _dot(xa, w3b[:half, :]) + _dot(xb, w3b[half:, :])
        o_ref[...] = _pack_rows(_dot((_silu(a) * b).astype(BF16), w2b[...]))

    @pl.when(c >= nu_ref[0])
    def _():
        o_ref[...] = jnp.zeros_like(o_ref)


def _expert_ffn(xs, chunk_e, n_used, w1, w3, w2, layer):
    n_slots, dp = xs.shape
    d = 2 * dp
    n_chunks = n_slots // EXPERT_CHUNK
    live = lambda c, nu: jnp.minimum(c, nu[0] - 1)
    wmap = lambda c, ce, nu: (layer, ce[live(c, nu)], 0, 0)
    grid_spec = pltpu.PrefetchScalarGridSpec(
        num_scalar_prefetch=2,
        grid=(n_chunks,),
        in_specs=[
            pl.BlockSpec((EXPERT_CHUNK, dp), lambda c, ce, nu: (live(c, nu), 0)),
            pl.BlockSpec((None, None, d, D_EXPERT), wmap),
            pl.BlockSpec((None, None, d, D_EXPERT), wmap),
            pl.BlockSpec((None, None, D_EXPERT, d), wmap),
        ],
        out_specs=pl.BlockSpec((EXPERT_CHUNK, dp), lambda c, ce, nu: (c, 0)),
        scratch_shapes=[pltpu.VMEM((d, D_EXPERT), BF16), pltpu.VMEM((d, D_EXPERT), BF16),
                        pltpu.VMEM((D_EXPERT, d), BF16)],
    )
    return pl.pallas_call(
        _expert_kernel, grid_spec=grid_spec,
        out_shape=jax.ShapeDtypeStruct((n_slots, dp), U32),
        compiler_params=_cparams("arbitrary"), name="moe_experts",
    )(chunk_e, n_used, xs, w1, w3, w2)


def _combine_kernel(sc_ref, sn_ref, w_ref, xs_ref, gt_ref, ys_hbm, o_ref, buf, sem):
    i = pl.program_id(0)
    n = pl.num_programs(0)
    slot = i % 2

    def issue(s_ref, s):
        for r in range(TC):
            for k in range(TOP_K):
                pltpu.make_async_copy(ys_hbm.at[pl.ds(s_ref[k, r], 1), :],
                                      buf.at[s, k, pl.ds(r, 1), :], sem.at[s]).start()

    @pl.when(i == 0)
    def _():
        issue(sc_ref, 0)

    for s in range(2):
        @pl.when(jnp.logical_and(i + 1 < n, 1 - slot == s))
        def _(s=s):
            issue(sn_ref, s)

    for k in range(TOP_K):
        pltpu.make_async_copy(ys_hbm.at[pl.ds(0, TC), :], buf.at[slot, k], sem.at[slot]).wait()
    dp = buf.shape[-1]
    acc_a = acc_b = None
    for k in range(TOP_K):
        wcol = jnp.transpose(jnp.broadcast_to(w_ref[k:k + 1, :], (LANES, TC)))
        wk = _lane_tile(wcol, dp)
        ya, yb = _unpack_rows(buf[slot, k])
        acc_a = ya * wk if acc_a is None else acc_a + ya * wk
        acc_b = yb * wk if acc_b is None else acc_b + yb * wk
    o_ref[...] = xs_ref[...] + gt_ref[...] * jnp.concatenate([acc_a, acc_b], axis=-1)


def _combine(slot_of, wts, xs, gtf, ys, bounds_tc):
    t, d = xs.shape
    nt = t // TC
    seq = lambda i: (_seq_of_tile(i, bounds_tc), 0, 0)
    return pl.pallas_call(
        _combine_kernel, grid=(nt,),
        in_specs=[pl.BlockSpec((TOP_K, TC), lambda i: (0, i), memory_space=pltpu.SMEM),
                  pl.BlockSpec((TOP_K, TC), lambda i: (0, jnp.minimum(i + 1, nt - 1)), memory_space=pltpu.SMEM),
                  pl.BlockSpec((TOP_K, TC), lambda i: (0, i)),
                  pl.BlockSpec((TC, d), lambda i: (i, 0)),
                  pl.BlockSpec((None, 1, d), seq),
                  pl.BlockSpec(memory_space=pl.ANY)],
        out_specs=pl.BlockSpec((TC, d), lambda i: (i, 0)),
        out_shape=jax.ShapeDtypeStruct((t, d), F32),
        scratch_shapes=[pltpu.VMEM((2, TOP_K, TC, ys.shape[1]), ys.dtype), pltpu.SemaphoreType.DMA((2,))],
        compiler_params=_cparams("arbitrary"), name="moe_combine",
    )(slot_of, slot_of, wts, xs, gtf, ys)


def _final_norm_kernel(x_ref, g_ref, o_ref):
    x = x_ref[...]
    ms = jnp.mean(x * x, axis=-1, keepdims=True)
    o_ref[...] = x * lax.rsqrt(ms + EPS) * g_ref[...]


def _final_norm(x, g, tile_off, n_tiles):
    d = x.shape[1]
    return pl.pallas_call(
        _final_norm_kernel, grid=(n_tiles,),
        in_specs=[pl.BlockSpec((TM, d), lambda i: (i + tile_off, 0)), pl.BlockSpec((1, d), lambda i: (0, 0))],
        out_specs=pl.BlockSpec((TM, d), lambda i: (i, 0)),
        out_shape=jax.ShapeDtypeStruct((n_tiles * TM, d), F32),
        compiler_params=_cparams("parallel"), name="final_norm",
    )(x, g)


def kernel(x_prompt, x_sample, c_prompt, c_sample, ada_w, ada_b, g_mix, g_ffn, g_final,
           w_in_ab, w_out_ab, conv_w, conv_b, filt_w1, filt_b1, filt_f1, filt_w2, filt_b2, filt_f2,
           filt_w3, decay, hyena_bias, w_qkv, w_out_c, rpb,
           router_w, router_b, exp_w1, exp_w3, exp_w2, sh_w1, sh_w3, sh_w2):
    d = D_MODEL
    bp, lp, _ = x_prompt.shape
    bs, ls, _ = x_sample.shape
    trunks = ((0, bp, lp), (bp * lp, bs, ls))
    t_all = bp * lp + bs * ls
    n_seq = bp + bs
    depth = ada_w.shape[0]
    assert lp % TM == 0 and ls % TM == 0
    bounds, acc = [], 0
    for _, b, l in trunks:
        for _ in range(b):
            acc += l // TM
            bounds.append(acc)
    bounds = tuple(bounds)
    bounds_tc = tuple(b * (TM // TC) for b in bounds)
    split_tile = bp * lp // TM

    x_pair = (x_prompt.reshape(bp * lp, d), x_sample.reshape(bs * ls, d))
    c_all = jnp.concatenate([c_prompt, c_sample, jnp.zeros((SUBLANES - n_seq, d), F32)], axis=0)
    mod = _modulation(c_all, ada_w, ada_b)[:, :n_seq]
    mod = mod.reshape(depth, n_seq, 6, 1, d)

    bands = jnp.linspace(1e-4, POS_BANDS - 1, POS_BANDS, dtype=F32).reshape(1, POS_BANDS)
    bias_tab = None

    for l in range(depth):
        sh_m, sc_m, gt_m, sh_f, sc_f, gt_f = (mod[l, :, j] for j in range(6))
        g_m = g_mix[l].reshape(1, d)
        i = l // 2
        if l % 2 == 0:
            u_f, u_h = _norm_mod_matmul(x_pair, g_m, sc_m, sh_m, w_in_ab[i].astype(BF16),
                                        (D_FNET, 3 * D_HYENA), (F32, F32), bounds, split_tile, "in_proj_ab")
            z, x0 = _hyena_pre(u_h, conv_w[i], conv_b[i], bounds)
            ya, yb = [], []
            for tok_off, b, sl in trunks:
                plan = _HyenaPlan(sl)
                hc, ss = _hyena_filter(sl, bands, filt_w1[i], filt_b1[i], filt_f1[i], filt_w2[i],
                                       filt_b2[i], filt_f2[i], filt_w3[i], decay[i])
                hr, hi = _hyena_spectrum(plan, hc)
                yb.append(_hyena_longconv(plan, z, x0, hyena_bias[i], hr, hi, ss, tok_off, b, sl))
                ya.append(_fnet_mix(u_f, tok_off, b, sl))
            parts = [tuple(ya), tuple(yb)]
            w_out = w_out_ab[i].astype(BF16)
            weights = [w_out[:D_FNET], w_out[D_FNET:]]
        else:
            q, k, v = _norm_mod_matmul(x_pair, g_m, sc_m, sh_m, w_qkv[i].astype(BF16),
                                       (d, d, d), (BF16, BF16, BF16), bounds, split_tile, "qkv_proj")
            bias_tab = _attn_bias_table(rpb[i])
            o = tuple(_attention(q, k, v, bias_tab, tok_off, b, sl) for tok_off, b, sl in trunks)
            parts = [o]
            weights = [w_out_c[i].astype(BF16)]
        s13 = jnp.concatenate([sh_w1[l], sh_w3[l]], axis=1).astype(BF16)
        xs, h, idx, wts, cnt_tile = _post_mixer(
            parts, weights, x_pair, gt_m, g_ffn[l].reshape(1, d), sc_f, sh_f, gt_f,
            router_w[l].T, router_b[l].reshape(N_EXPERTS, 1), s13, sh_w2[l].astype(BF16),
            bounds, split_tile, "post_mixer_%d" % l)
        base, chunk_e, n_used, pad_lo, pad_hi, n_chunks = _dispatch_tables(cnt_tile)
        slot_of = _rank(idx, base)
        xg = _push(slot_of, h, pad_lo, pad_hi, n_used, n_chunks * EXPERT_CHUNK)
        ys = _expert_ffn(xg, chunk_e, n_used, exp_w1, exp_w3, exp_w2, l)
        x = _combine(slot_of, wts, xs, gt_f, ys, bounds_tc)
        x_pair = (x, x)

    g_fin = g_final.reshape(1, d)
    y_prompt = _final_norm(x, g_fin, 0, bp * lp // TM).reshape(bp, lp, d)
    y_sample = _final_norm(x, g_fin, bp * lp // TM, bs * ls // TM).reshape(bs, ls, d)
    return (y_prompt, y_sample)
```
